```python
import math
import jax
import jax.numpy as jnp
from jax import lax
import numpy as np

D_MODEL = 4096
BATCH = 8
SEQ = 2048
DEPTH = 2
DEC_BATCH = 4
DEC_SEQ = 2048
PAST_LEN = 128

SSD_WIDTH = D_MODEL
SSD_HEAD_DIM = 64
SSD_HEADS = SSD_WIDTH // SSD_HEAD_DIM
SSD_GROUPS = 8
SSD_HEADS_PER_GROUP = SSD_HEADS // SSD_GROUPS
SSD_STATE = 128
SSD_CONV = 5
SSD_CHUNK = 128
XBC_WIDTH = SSD_WIDTH + 2 * SSD_GROUPS * SSD_STATE
CM_WIDTH = D_MODEL
CM_CONV = 31
MIX_WIDTH = SSD_WIDTH + CM_WIDTH
IN_WIDTH = SSD_WIDTH + XBC_WIDTH + 2 * SSD_HEADS + 2 * CM_WIDTH
D_FF = ((8 * D_MODEL // 3 + 255) // 256) * 256
DEEPNORM_ALPHA = (2 * DEPTH) ** 0.25
DEEPNORM_BETA = (8 * DEPTH) ** -0.25
LN_EPS = 1e-5
RMS_EPS = 1e-5

kernel_name = 'hymba_ssd_conformer_encoder'


def _layernorm(x, g, b):
    xf = x.astype(jnp.float32)
    mu = jnp.mean(xf, axis=-1, keepdims=True)
    var = jnp.mean(jnp.square(xf - mu), axis=-1, keepdims=True)
    y = (xf - mu) * lax.rsqrt(var + LN_EPS) * g.astype(jnp.float32) + b.astype(jnp.float32)
    return y.astype(x.dtype)


def _swiglu(x, w_gate, w_up, w_down):
    return (jax.nn.silu(x @ w_gate) * (x @ w_up)) @ w_down


def _dwconv(x, w, b):
    k = w.shape[0]
    pad = (k - 1) // 2
    y = lax.conv_general_dilated(
        x, w[:, None, :], window_strides=(1,), padding=[(pad, k - 1 - pad)],
        dimension_numbers=('NWC', 'WIO', 'NWC'), feature_group_count=x.shape[-1])
    return y + b


def _ssd_chunked(x, dt, a, b_in, c_in):
    bsz, seqlen, g, r, p = x.shape
    n = b_in.shape[-1]
    nc = seqlen // SSD_CHUNK
    xdt = (x * dt[..., None]).reshape(bsz, nc, SSD_CHUNK, g, r, p)
    a_cs = jnp.cumsum((dt * a).reshape(bsz, nc, SSD_CHUNK, g, r), axis=2)
    bc = b_in.reshape(bsz, nc, SSD_CHUNK, g, n)
    cc = c_in.reshape(bsz, nc, SSD_CHUNK, g, n)
    mask = jnp.tril(jnp.ones((SSD_CHUNK, SSD_CHUNK), dtype=bool))[:, :, None, None]
    seg = a_cs[:, :, :, None] - a_cs[:, :, None, :]
    decay = jnp.exp(jnp.where(mask, seg, -jnp.inf))
    scores = jnp.einsum('bclgn,bcsgn->bclsg', cc, bc)
    y_diag = jnp.einsum('bclsg,bclsgr,bcsgrp->bclgrp', scores, decay, xdt)
    decay_to_end = jnp.exp(a_cs[:, :, -1:] - a_cs)
    states = jnp.einsum('bclgn,bclgr,bclgrp->bcgrpn', bc, decay_to_end, xdt)
    chunk_decay = jnp.exp(a_cs[:, :, -1])

    def step(h, inp):
        st, dec = inp
        return h * dec[..., None, None] + st, h

    h0 = jnp.zeros((bsz, g, r, p, n), dtype=x.dtype)
    _, prev = lax.scan(step, h0, (jnp.moveaxis(states, 1, 0), jnp.moveaxis(chunk_decay, 1, 0)))
    prev = jnp.moveaxis(prev, 0, 1)
    y_off = jnp.einsum('bclgn,bcgrpn,bclgr->bclgrp', cc, prev, jnp.exp(a_cs))
    return (y_diag + y_off).reshape(bsz, seqlen, g, r, p)


def _gated_rmsnorm(y, z, w):
    gy = y * jax.nn.silu(z.astype(jnp.float32))
    shp = gy.shape
    gg = gy.reshape(shp[:-1] + (SSD_GROUPS, shp[-1] // SSD_GROUPS))
    gg = gg * lax.rsqrt(jnp.mean(gg * gg, axis=-1, keepdims=True) + RMS_EPS)
    return gg.reshape(shp) * w.astype(jnp.float32)


def _mixer(x, w_in, ssd_conv_w, ssd_conv_b, dt_bias_fwd, dt_bias_bwd, a_log_fwd, a_log_bwd, d_skip,
           ssd_norm_w, cm_conv_w, cm_conv_b, cm_ln_g, cm_ln_b, w_out):
    f32 = jnp.float32
    g, r, p, n = SSD_GROUPS, SSD_HEADS_PER_GROUP, SSD_HEAD_DIM, SSD_STATE
    bsz, seqlen, _ = x.shape
    proj = x @ w_in
    z, xbc, dt_raw, cm_in = jnp.split(
        proj, [SSD_WIDTH, SSD_WIDTH + XBC_WIDTH, SSD_WIDTH + XBC_WIDTH + 2 * SSD_HEADS], axis=-1)
    xbc = jax.nn.silu(_dwconv(xbc, ssd_conv_w, ssd_conv_b))
    xs, b_in, c_in = jnp.split(xbc, [SSD_WIDTH, SSD_WIDTH + g * n], axis=-1)
    xs = xs.astype(f32).reshape(bsz, seqlen, g, r, p)
    b_in = b_in.astype(f32).reshape(bsz, seqlen, g, n)
    c_in = c_in.astype(f32).reshape(bsz, seqlen, g, n)
    dt = dt_raw.astype(f32).reshape(bsz, seqlen, 2, g, r)
    dt_f = jax.nn.softplus(dt[:, :, 0] + dt_bias_fwd.astype(f32).reshape(g, r))
    dt_b = jax.nn.softplus(dt[:, :, 1] + dt_bias_bwd.astype(f32).reshape(g, r))
    a_f = -jnp.exp(a_log_fwd.astype(f32)).reshape(g, r)
    a_b = -jnp.exp(a_log_bwd.astype(f32)).reshape(g, r)
    y_fwd = _ssd_chunked(xs, dt_f, a_f, b_in, c_in)
    fl = lambda t: jnp.flip(t, axis=1)
    y_bwd = fl(_ssd_chunked(fl(xs), fl(dt_b), a_b, fl(b_in), fl(c_in)))
    y = y_fwd + y_bwd + d_skip.astype(f32).reshape(g, r, 1) * xs
    y = y.reshape(bsz, seqlen, SSD_WIDTH)
    ssd_out = _gated_rmsnorm(y, z, ssd_norm_w).astype(x.dtype)
    cm_val, cm_gate = jnp.split(cm_in, 2, axis=-1)
    u = cm_val * jax.nn.sigmoid(cm_gate)
    u = _dwconv(u, cm_conv_w, cm_conv_b)
    u = jax.nn.silu(_layernorm(u, cm_ln_g, cm_ln_b))
    return jnp.concatenate([ssd_out, u], axis=-1) @ w_out


def _layer(x, ffn1_w_gate, ffn1_w_up, ffn1_w_down, ffn1_ln_g, ffn1_ln_b,
           w_in, ssd_conv_w, ssd_conv_b, dt_bias_fwd, dt_bias_bwd, a_log_fwd, a_log_bwd, d_skip, ssd_norm_w,
           cm_conv_w, cm_conv_b, cm_ln_g, cm_ln_b, w_out, mix_ln_g, mix_ln_b,
           ffn2_w_gate, ffn2_w_up, ffn2_w_down, ffn2_ln_g, ffn2_ln_b):
    x = _layernorm(DEEPNORM_ALPHA * x + 0.5 * _swiglu(x, ffn1_w_gate, ffn1_w_up, ffn1_w_down), ffn1_ln_g, ffn1_ln_b)
    m = _mixer(x, w_in, ssd_conv_w, ssd_conv_b, dt_bias_fwd, dt_bias_bwd, a_log_fwd, a_log_bwd, d_skip,
               ssd_norm_w, cm_conv_w, cm_conv_b, cm_ln_g, cm_ln_b, w_out)
    x = _layernorm(DEEPNORM_ALPHA * x + m, mix_ln_g, mix_ln_b)
    x = _layernorm(DEEPNORM_ALPHA * x + 0.5 * _swiglu(x, ffn2_w_gate, ffn2_w_up, ffn2_w_down), ffn2_ln_g, ffn2_ln_b)
    return x


def _trunk(x, params):
    for l in range(DEPTH):
        x = _layer(x, *[prm[l] for prm in params])
    return x


def setup_inputs(seed: int = 0) -> dict:
    key = jax.random.key(seed)
    ks = jax.random.split(key, 32)
    L = DEPTH
    f32 = jnp.float32

    def nrm(k, shape, std):
        return jax.random.normal(k, shape, f32) * std

    def gain(k, shape):
        return 1.0 + 0.02 * jax.random.normal(k, shape, f32)

    def bias(k, shape):
        return 0.02 * jax.random.normal(k, shape, f32)

    dt_init_f = jnp.exp(jax.random.uniform(ks[10], (L, SSD_HEADS), f32, math.log(1e-3), math.log(1e-1)))
    dt_init_b = jnp.exp(jax.random.uniform(ks[11], (L, SSD_HEADS), f32, math.log(1e-3), math.log(1e-1)))
    return {
        'x_prompt': nrm(ks[0], (BATCH, SEQ, D_MODEL), 1.0),
        'x_sample': nrm(ks[1], (DEC_BATCH, DEC_SEQ, D_MODEL), 1.0),
        'ffn1_w_gate': nrm(ks[2], (L, D_MODEL, D_FF), D_MODEL ** -0.5),
        'ffn1_w_up': nrm(ks[3], (L, D_MODEL, D_FF), D_MODEL ** -0.5),
        'ffn1_w_down': nrm(ks[4], (L, D_FF, D_MODEL), DEEPNORM_BETA * D_FF ** -0.5),
        'ffn1_ln_g': gain(ks[5], (L, D_MODEL)),
        'ffn1_ln_b': bias(ks[6], (L, D_MODEL)),
        'w_in': nrm(ks[7], (L, D_MODEL, IN_WIDTH), D_MODEL ** -0.5),
        'ssd_conv_w': nrm(ks[8], (L, SSD_CONV, XBC_WIDTH), SSD_CONV ** -0.5),
        'ssd_conv_b': bias(ks[9], (L, XBC_WIDTH)),
        'dt_bias_fwd': dt_init_f + jnp.log(-jnp.expm1(-dt_init_f)),
        'dt_bias_bwd': dt_init_b + jnp.log(-jnp.expm1(-dt_init_b)),
        'a_log_fwd': jnp.log(jax.random.uniform(ks[12], (L, SSD_HEADS), f32, 1.0, 16.0)),
        'a_log_bwd': jnp.log(jax.random.uniform(ks[13], (L, SSD_HEADS), f32, 1.0, 16.0)),
        'd_skip': gain(ks[14], (L, SSD_HEADS)),
        'ssd_norm_w': gain(ks[15], (L, SSD_WIDTH)),
        'cm_conv_w': nrm(ks[16], (L, CM_CONV, CM_WIDTH), CM_CONV ** -0.5),
        'cm_conv_b': bias(ks[17], (L, CM_WIDTH)),
        'cm_ln_g': gain(ks[18], (L, CM_WIDTH)),
        'cm_ln_b': bias(ks[19], (L, CM_WIDTH)),
        'w_out': nrm(ks[20], (L, MIX_WIDTH, D_MODEL), DEEPNORM_BETA * MIX_WIDTH ** -0.5),
        'mix_ln_g': gain(ks[21], (L, D_MODEL)),
        'mix_ln_b': bias(ks[22], (L, D_MODEL)),
        'ffn2_w_gate': nrm(ks[23], (L, D_MODEL, D_FF), D_MODEL ** -0.5),
        'ffn2_w_up': nrm(ks[24], (L, D_MODEL, D_FF), D_MODEL ** -0.5),
        'ffn2_w_down': nrm(ks[25], (L, D_FF, D_MODEL), DEEPNORM_BETA * D_FF ** -0.5),
        'ffn2_ln_g': gain(ks[26], (L, D_MODEL)),
        'ffn2_ln_b': bias(ks[27], (L, D_MODEL)),
    }


def reference(x_prompt, x_sample, ffn1_w_gate, ffn1_w_up, ffn1_w_down, ffn1_ln_g, ffn1_ln_b,
              w_in, ssd_conv_w, ssd_conv_b, dt_bias_fwd, dt_bias_bwd, a_log_fwd, a_log_bwd, d_skip, ssd_norm_w,
              cm_conv_w, cm_conv_b, cm_ln_g, cm_ln_b, w_out, mix_ln_g, mix_ln_b,
              ffn2_w_gate, ffn2_w_up, ffn2_w_down, ffn2_ln_g, ffn2_ln_b):
    params = (ffn1_w_gate, ffn1_w_up, ffn1_w_down, ffn1_ln_g, ffn1_ln_b,
              w_in, ssd_conv_w, ssd_conv_b, dt_bias_fwd, dt_bias_bwd, a_log_fwd, a_log_bwd, d_skip, ssd_norm_w,
              cm_conv_w, cm_conv_b, cm_ln_g, cm_ln_b, w_out, mix_ln_g, mix_ln_b,
              ffn2_w_gate, ffn2_w_up, ffn2_w_down, ffn2_ln_g, ffn2_ln_b)
    y_prompt = _trunk(x_prompt, params)
    y_sample = _trunk(x_sample, params)
    return (y_prompt, y_sample)
```

```python
import functools

import jax
import jax.numpy as jnp
import numpy as np
from jax import lax
from jax.experimental import pallas as pl
from jax.experimental.pallas import tpu as pltpu

F32 = jnp.float32
BF16 = jnp.bfloat16

HEAD_DIM = 64
GROUPS = 8
HEADS_PER_GROUP = 8
GROUP_WIDTH = HEAD_DIM * HEADS_PER_GROUP
STATE = 128
SSD_CONV = 5
CHUNK = 128
CM_CONV = 31
LN_EPS = 1e-5
RMS_EPS = 1e-5

LANES = 128
SUBLANES = 8
VMEM_LIMIT_BYTES = 60 * 1024 * 1024

FFN_ROWS = 512
FFN_COLS = 256
PROJ_ROWS = 1024
PROJ_COLS = 1024
OUT_ROWS = 512
OUT_K = 512
CM_ROWS = 256
CM_HALO = 16
LN_ROW_BLOCK = 32


def _params(sem):
    return pltpu.CompilerParams(dimension_semantics=sem, vmem_limit_bytes=VMEM_LIMIT_BYTES)


def _silu(v):
    return v * jax.nn.sigmoid(v)


def _layernorm_store(y, g, b, o32_ref, ob_ref, rows):
    mu = jnp.mean(y, axis=-1, keepdims=True)
    d = y - mu
    var = jnp.mean(d * d, axis=-1, keepdims=True)
    out = d * lax.rsqrt(var + LN_EPS) * g + b
    o32_ref[rows, :] = out
    ob_ref[rows, :] = out.astype(BF16)


def _ffn_kernel(x32_ref, xb_ref, wg_ref, wu_ref, wd_ref, g_ref, b_ref, o32_ref, ob_ref, *, alpha):
    j = pl.program_id(1)

    @pl.when(j == 0)
    def _():
        o32_ref[...] = jnp.zeros_like(o32_ref)

    xb = xb_ref[...]
    gate = jnp.dot(xb, wg_ref[...], preferred_element_type=F32)
    up = jnp.dot(xb, wu_ref[...], preferred_element_type=F32)
    h = (_silu(gate) * up).astype(BF16)
    o32_ref[...] += jnp.dot(h, wd_ref[...], preferred_element_type=F32)

    @pl.when(j == pl.num_programs(1) - 1)
    def _():
        g = g_ref[...]
        b = b_ref[...]

        def body(r, carry):
            rows = pl.ds(pl.multiple_of(r * LN_ROW_BLOCK, LN_ROW_BLOCK), LN_ROW_BLOCK)
            y = alpha * x32_ref[rows, :] + 0.5 * o32_ref[rows, :]
            _layernorm_store(y, g, b, o32_ref, ob_ref, rows)
            return carry

        lax.fori_loop(0, o32_ref.shape[0] // LN_ROW_BLOCK, body, 0)


def _ffn_ln(x32, xb, wg, wu, wd, g, b, layer, alpha):
    t, d = x32.shape
    f = wg.shape[-1]
    tm = min(FFN_ROWS, t)
    tf = FFN_COLS
    grid = (t // tm, f // tf)
    single = pl.Buffered(1)
    return pl.pallas_call(
        functools.partial(_ffn_kernel, alpha=alpha),
        grid=grid,
        in_specs=[
            pl.BlockSpec((tm, d), lambda i, j: (i, 0), pipeline_mode=single),
            pl.BlockSpec((tm, d), lambda i, j: (i, 0)),
            pl.BlockSpec((None, d, tf), lambda i, j: (layer, 0, j)),
            pl.BlockSpec((None, d, tf), lambda i, j: (layer, 0, j)),
            pl.BlockSpec((None, tf, d), lambda i, j: (layer, j, 0)),
            pl.BlockSpec((None, 1, d), lambda i, j: (layer, 0, 0)),
            pl.BlockSpec((None, 1, d), lambda i, j: (layer, 0, 0)),
        ],
        out_specs=[
            pl.BlockSpec((tm, d), lambda i, j: (i, 0), pipeline_mode=single),
            pl.BlockSpec((tm, d), lambda i, j: (i, 0), pipeline_mode=single),
        ],
        out_shape=[jax.ShapeDtypeStruct((t, d), F32), jax.ShapeDtypeStruct((t, d), BF16)],
        compiler_params=_params(("parallel", "arbitrary")),
        name="ffn_ln",
    )(x32, xb, wg, wu, wd, g, b)


def _inproj_kernel(xb_ref, w_ref, wdt_ref, o_ref, odt_ref):
    xb = xb_ref[...]
    o_ref[...] = jnp.dot(xb, w_ref[...], preferred_element_type=F32)

    @pl.when(pl.program_id(1) == 0)
    def _():
        odt_ref[...] = jnp.dot(xb, wdt_ref[...], preferred_element_type=F32)


def _inproj(xb, w_main, w_dt, layer):
    t, d = xb.shape
    n = w_main.shape[-1]
    ndt = w_dt.shape[-1]
    tm = min(PROJ_ROWS, t)
    tn = PROJ_COLS
    return pl.pallas_call(
        _inproj_kernel,
        grid=(t // tm, n // tn),
        in_specs=[
            pl.BlockSpec((tm, d), lambda i, j: (i, 0)),
            pl.BlockSpec((None, d, tn), lambda i, j: (layer, 0, j)),
            pl.BlockSpec((None, d, ndt), lambda i, j: (layer, 0, 0)),
        ],
        out_specs=[
            pl.BlockSpec((tm, tn), lambda i, j: (i, j)),
            pl.BlockSpec((tm, ndt), lambda i, j: (i, 0)),
        ],
        out_shape=[jax.ShapeDtypeStruct((t, n), F32), jax.ShapeDtypeStruct((t, ndt), F32)],
        compiler_params=_params(("parallel", "arbitrary")),
        name="inproj",
    )(xb, w_main, w_dt)


SSD_PAD = 8
SSD_CONV_ROWS = 32
SSD_OUT_ROWS = 64


def _ssd_kernel(xs_ref, bm_ref, cm_ref, z_ref, dt_ref, cwx_ref, cwb_ref, cwc_ref,
                cbx_ref, cbb_ref, cbc_ref, dtb_ref, alog_ref, dskip_ref, nw_ref,
                o_ref,
                pad_s, xs_s, bf_s, bc_s, bt_s, cc_s, dt_s, acs_s, rowf_s, y_s, h_s):
    seq = xs_ref.shape[0]
    nchunks = seq // CHUNK
    grp = pl.program_id(1)
    half = (SSD_CONV - 1) // 2

    def conv_silu(src_ref, w_ref, b_ref, width, store):
        pad_s[0:SSD_PAD, 0:width] = jnp.zeros((SSD_PAD, width), F32)
        pad_s[SSD_PAD + seq:2 * SSD_PAD + seq, 0:width] = jnp.zeros((SSD_PAD, width), F32)

        def copy_body(r, carry):
            r0 = pl.multiple_of(r * SSD_CONV_ROWS, SSD_CONV_ROWS)
            pad_s[pl.ds(SSD_PAD + r0, SSD_CONV_ROWS), 0:width] = src_ref[pl.ds(r0, SSD_CONV_ROWS), :]
            return carry

        lax.fori_loop(0, seq // SSD_CONV_ROWS, copy_body, 0)

        def conv_body(r, carry):
            r0 = pl.multiple_of(r * SSD_CONV_ROWS, SSD_CONV_ROWS)
            win = pad_s[pl.ds(r0, SSD_CONV_ROWS + 2 * SSD_PAD), 0:width]
            acc = jnp.broadcast_to(b_ref[...], (SSD_CONV_ROWS, width))
            for k in range(SSD_CONV):
                off = SSD_PAD - half + k
                acc = acc + w_ref[k:k + 1, :] * win[off:off + SSD_CONV_ROWS, :]
            store(r0, _silu(acc))
            return carry

        lax.fori_loop(0, seq // SSD_CONV_ROWS, conv_body, 0)

    def store_x(r0, v):
        xs_s[pl.ds(r0, SSD_CONV_ROWS), :] = v

    def store_b(r0, v):
        bf_s[pl.ds(r0, SSD_CONV_ROWS), :] = v

    def store_c(r0, v):
        cc_s[pl.ds(r0, SSD_CONV_ROWS), :] = v.astype(BF16)

    conv_silu(xs_ref, cwx_ref, cbx_ref, GROUP_WIDTH, store_x)
    conv_silu(bm_ref, cwb_ref, cbb_ref, STATE, store_b)
    conv_silu(cm_ref, cwc_ref, cbc_ref, STATE, store_c)

    row_id = lax.broadcasted_iota(jnp.int32, (CHUNK, LANES), 0)
    lane_id = lax.broadcasted_iota(jnp.int32, (CHUNK, LANES), 1)
    lane_shift = lax.rem(LANES - 2 * HEADS_PER_GROUP * grp, LANES)
    neg_a = -jnp.exp(alog_ref[...])

    def prep_body(c, carry):
        rows = pl.ds(pl.multiple_of(c * CHUNK, CHUNK), CHUNK)
        raw = dt_ref[rows, :] + dtb_ref[...]
        dt = jnp.maximum(raw, 0.0) + jnp.log1p(jnp.exp(-jnp.abs(raw)))
        da = dt * neg_a
        dt = pltpu.roll(dt, lane_shift, 1)
        da = pltpu.roll(da, lane_shift, 1)
        fwd = da
        bwd = da
        step = 1
        while step < CHUNK:
            fwd = fwd + jnp.where(row_id >= step, pltpu.roll(fwd, step, 0), 0.0)
            bwd = bwd + jnp.where(row_id < CHUNK - step, pltpu.roll(bwd, CHUNK - step, 0), 0.0)
            step *= 2
        acs = jnp.where(lane_id < HEADS_PER_GROUP, fwd, bwd)
        dt_s[rows, :] = dt
        acs_s[rows, :] = acs
        rowf_s[c] = acs.T[0:2 * HEADS_PER_GROUP, :]
        bchunk = bf_s[rows, :]
        bc_s[rows, :] = bchunk.astype(BF16)
        bt_s[c] = bchunk.T.astype(BF16)
        return carry

    lax.fori_loop(0, nchunks, prep_body, 0)

    def expand(cols, k0):
        lane = lax.broadcasted_iota(jnp.int32, (cols.shape[0], LANES), 1)
        parts = []
        for j in range(HEADS_PER_GROUP // 2):
            a = cols[:, k0 + 2 * j:k0 + 2 * j + 1]
            b = cols[:, k0 + 2 * j + 1:k0 + 2 * j + 2]
            parts.append(jnp.where(lane < HEAD_DIM, a, b))
        return jnp.concatenate(parts, axis=1)

    def scan_direction(backward):
        k0 = HEADS_PER_GROUP if backward else 0
        keep = (row_id <= lane_id) if backward else (row_id >= lane_id)
        h_s[...] = jnp.zeros_like(h_s)

        def chunk_body(i, carry):
            c = (nchunks - 1 - i) if backward else i
            rows = pl.ds(pl.multiple_of(c * CHUNK, CHUNK), CHUNK)
            cc = cc_s[rows, :]
            bc = bc_s[rows, :]
            scores = lax.dot_general(cc, bc, (((1,), (1,)), ((), ())), preferred_element_type=F32)
            xs_c = xs_s[rows, :]
            dt16 = dt_s[rows, :]
            acs = acs_s[rows, :]
            rowf = rowf_s[c]
            last = acs[0:1, :] if backward else acs[CHUNK - 1:CHUNK, :]
            xdt = xs_c * expand(dt16, k0)
            w_in = (xdt * expand(jnp.exp(last - acs), k0)).astype(BF16)
            y = jnp.dot(cc, h_s[...].astype(BF16), preferred_element_type=F32) * expand(jnp.exp(acs), k0)
            parts = []
            for j in range(HEADS_PER_GROUP // 2):
                ms = []
                for h in (2 * j, 2 * j + 1):
                    seg = acs[:, k0 + h:k0 + h + 1] - rowf[k0 + h:k0 + h + 1, :]
                    decay = jnp.exp(jnp.where(keep, seg, -jnp.inf))
                    ms.append((scores * decay).astype(BF16))
                lhs = jnp.concatenate(ms, axis=1)
                pair = xdt[:, j * LANES:(j + 1) * LANES]
                rhs = jnp.concatenate([jnp.where(lane_id < HEAD_DIM, pair, 0.0).astype(BF16),
                                       jnp.where(lane_id >= HEAD_DIM, pair, 0.0).astype(BF16)], axis=0)
                parts.append(jnp.dot(lhs, rhs, preferred_element_type=F32))
            y = y + jnp.concatenate(parts, axis=1)
            if backward:
                y_s[rows, :] += y
            else:
                y_s[rows, :] = y
            h_s[...] = h_s[...] * expand(jnp.exp(last), k0) + jnp.dot(
                bt_s[c], w_in, preferred_element_type=F32)
            return carry

        lax.fori_loop(0, nchunks, chunk_body, 0)

    scan_direction(False)
    scan_direction(True)

    def out_body(r, carry):
        rows = pl.ds(pl.multiple_of(r * SSD_OUT_ROWS, SSD_OUT_ROWS), SSD_OUT_ROWS)
        y = y_s[rows, :] + dskip_ref[...] * xs_s[rows, :]
        gy = y * _silu(z_ref[rows, :])
        ms = jnp.mean(gy * gy, axis=-1, keepdims=True)
        o_ref[rows, :] = (gy * lax.rsqrt(ms + RMS_EPS) * nw_ref[...]).astype(BF16)
        return carry

    lax.fori_loop(0, seq // SSD_OUT_ROWS, out_body, 0)


def _ssd(proj3, dt3, conv_w, conv_b, dtb, alog, dskip, norm_w, layer, col):
    nb, seq, _ = proj3.shape
    nchunks = seq // CHUNK
    gw, st = GROUP_WIDTH, STATE
    width = GROUPS * gw
    xs_blk, b_blk, c_blk, z_blk = col["xs"] // gw, col["b"] // st, col["c"] // st, col["z"] // gw
    cb_blk, cc_blk = width // st, (width + GROUPS * st) // st
    return pl.pallas_call(
        _ssd_kernel,
        grid=(nb, GROUPS),
        in_specs=[
            pl.BlockSpec((None, seq, gw), lambda b, g: (b, 0, xs_blk + g)),
            pl.BlockSpec((None, seq, st), lambda b, g: (b, 0, b_blk + g)),
            pl.BlockSpec((None, seq, st), lambda b, g: (b, 0, c_blk + g)),
            pl.BlockSpec((None, seq, gw), lambda b, g: (b, 0, z_blk + g)),
            pl.BlockSpec((None, seq, LANES), lambda b, g: (b, 0, 0)),
            pl.BlockSpec((None, SSD_CONV, gw), lambda b, g: (layer, 0, g)),
            pl.BlockSpec((None, SSD_CONV, st), lambda b, g: (layer, 0, cb_blk + g)),
            pl.BlockSpec((None, SSD_CONV, st), lambda b, g: (layer, 0, cc_blk + g)),
            pl.BlockSpec((None, 1, gw), lambda b, g: (layer, 0, g)),
            pl.BlockSpec((None, 1, st), lambda b, g: (layer, 0, cb_blk + g)),
            pl.BlockSpec((None, 1, st), lambda b, g: (layer, 0, cc_blk + g)),
            pl.BlockSpec((None, 1, LANES), lambda b, g: (layer, 0, 0)),
            pl.BlockSpec((None, 1, LANES), lambda b, g: (layer, 0, 0)),
            pl.BlockSpec((None, 1, gw), lambda b, g: (layer, 0, g)),
            pl.BlockSpec((None, 1, gw), lambda b, g: (layer, 0, g)),
        ],
        out_specs=pl.BlockSpec((None, seq, gw), lambda b, g: (b, 0, g)),
        out_shape=jax.ShapeDtypeStruct((nb, seq, width), BF16),
        scratch_shapes=[
            pltpu.VMEM((seq + 2 * SSD_PAD, gw), F32),
            pltpu.VMEM((seq, gw), F32),
            pltpu.VMEM((seq, st), F32),
            pltpu.VMEM((seq, st), BF16),
            pltpu.VMEM((nchunks, st, CHUNK), BF16),
            pltpu.VMEM((seq, st), BF16),
            pltpu.VMEM((seq, LANES), F32),
            pltpu.VMEM((seq, LANES), F32),
            pltpu.VMEM((nchunks, 2 * HEADS_PER_GROUP, CHUNK), F32),
            pltpu.VMEM((seq, gw), F32),
            pltpu.VMEM((st, gw), F32),
        ],
        compiler_params=_params(("parallel", "arbitrary")),
        name="ssd",
    )(proj3, proj3, proj3, proj3, dt3, conv_w, conv_w, conv_w, conv_b, conv_b, conv_b,
      dtb, alog, dskip, norm_w)


CM_CONV_ROWS = 64
CM_CONV_LANES = 512
CM_SHIFT_ROWS = 32


def _convmod_kernel(v_ref, g_ref, vp_ref, gp_ref, vn_ref, gn_ref, w_ref, b_ref, lg_ref, lb_ref,
                    o_ref, u_s, c_s, sh_s):
    tt, width = v_ref.shape
    t = pl.program_id(1)
    nt = pl.num_programs(1)
    half = (CM_CONV - 1) // 2

    def glu(v, g):
        return v * jax.nn.sigmoid(g)

    prev = glu(vp_ref[...], gp_ref[...])
    u_s[0:CM_HALO, :] = jnp.where(t > 0, prev, jnp.zeros_like(prev))
    nxt = glu(vn_ref[...], gn_ref[...])
    u_s[CM_HALO + tt:2 * CM_HALO + tt, :] = jnp.where(t < nt - 1, nxt, jnp.zeros_like(nxt))

    def glu_body(r, carry):
        r0 = pl.multiple_of(r * LN_ROW_BLOCK, LN_ROW_BLOCK)
        rows = pl.ds(r0, LN_ROW_BLOCK)
        u_s[pl.ds(CM_HALO + r0, LN_ROW_BLOCK), :] = glu(v_ref[rows, :], g_ref[rows, :])
        return carry

    lax.fori_loop(0, tt // LN_ROW_BLOCK, glu_body, 0)

    u_s[2 * CM_HALO + tt:2 * CM_HALO + tt + SUBLANES, :] = jnp.zeros((SUBLANES, width), F32)

    for cb in range(width // CM_CONV_LANES):
        lanes = slice(cb * CM_CONV_LANES, (cb + 1) * CM_CONV_LANES)

        def shift_body(r, carry, lanes=lanes):
            r0 = pl.multiple_of(r * CM_SHIFT_ROWS, CM_SHIFT_ROWS)
            win = u_s[pl.ds(r0, CM_SHIFT_ROWS + SUBLANES), lanes]
            for s in range(SUBLANES):
                sh_s[s, pl.ds(r0, CM_SHIFT_ROWS), :] = win[s:s + CM_SHIFT_ROWS, :]
            return carry

        lax.fori_loop(0, (tt + 2 * CM_HALO) // CM_SHIFT_ROWS, shift_body, 0)

        def conv_body(r, carry, lanes=lanes):
            r0 = pl.multiple_of(r * CM_CONV_ROWS, CM_CONV_ROWS)
            acc = jnp.broadcast_to(b_ref[:, lanes], (CM_CONV_ROWS, CM_CONV_LANES))
            for k in range(CM_CONV):
                off = CM_HALO - half + k
                win = sh_s[off % SUBLANES, pl.ds(r0 + (off // SUBLANES) * SUBLANES, CM_CONV_ROWS), :]
                acc = acc + w_ref[k:k + 1, lanes] * win
            c_s[pl.ds(r0, CM_CONV_ROWS), lanes] = acc
            return carry

        lax.fori_loop(0, tt // CM_CONV_ROWS, conv_body, 0)

    lg = lg_ref[...]
    lb = lb_ref[...]

    def ln_body(r, carry):
        rows = pl.ds(pl.multiple_of(r * LN_ROW_BLOCK, LN_ROW_BLOCK), LN_ROW_BLOCK)
        y = c_s[rows, :]
        mu = jnp.mean(y, axis=-1, keepdims=True)
        d = y - mu
        var = jnp.mean(d * d, axis=-1, keepdims=True)
        o_ref[rows, :] = _silu(d * lax.rsqrt(var + LN_EPS) * lg + lb).astype(BF16)
        return carry

    lax.fori_loop(0, tt // LN_ROW_BLOCK, ln_body, 0)


def _convmod(proj3, conv_w, conv_b, ln_g, ln_b, layer, col, rows_per_step):
    nb, seq, _ = proj3.shape
    width = conv_w.shape[-1]
    tt = min(rows_per_step, seq)
    halo_per_tile = tt // CM_HALO
    last_halo = seq // CM_HALO - 1
    v_blk, g_blk = col["cm_val"] // width, col["cm_gate"] // width

    def cur(blk):
        return pl.BlockSpec((None, tt, width), lambda b, t: (b, t, blk))

    def before(blk):
        return pl.BlockSpec((None, CM_HALO, width),
                            lambda b, t: (b, jnp.maximum(t * halo_per_tile - 1, 0), blk))

    def after(blk):
        return pl.BlockSpec((None, CM_HALO, width),
                            lambda b, t: (b, jnp.minimum((t + 1) * halo_per_tile, last_halo), blk))

    def per_layer(rows):
        return pl.BlockSpec((None, rows, width), lambda b, t: (layer, 0, 0))

    return pl.pallas_call(
        _convmod_kernel,
        grid=(nb, seq // tt),
        in_specs=[cur(v_blk), cur(g_blk), before(v_blk), before(g_blk), after(v_blk), after(g_blk),
                  per_layer(CM_CONV), per_layer(1), per_layer(1), per_layer(1)],
        out_specs=pl.BlockSpec((None, tt, width), lambda b, t: (b, t, 0)),
        out_shape=jax.ShapeDtypeStruct((nb, seq, width), BF16),
        scratch_shapes=[
            pltpu.VMEM((tt + 2 * CM_HALO + SUBLANES, width), F32),
            pltpu.VMEM((tt, width), F32),
            pltpu.VMEM((SUBLANES, tt + 2 * CM_HALO, CM_CONV_LANES), F32),
        ],
        compiler_params=_params(("parallel", "arbitrary")),
        name="convmod",
    )(proj3, proj3, proj3, proj3, proj3, proj3, conv_w, conv_b, ln_g, ln_b)


def _outproj_kernel(x32_ref, a_ref, u_ref, w_ref, g_ref, b_ref, o32_ref, ob_ref, *, alpha, ksplit):
    k = pl.program_id(1)

    @pl.when(k == 0)
    def _():
        o32_ref[...] = jnp.zeros_like(o32_ref)

    lhs = jnp.where(k < ksplit, a_ref[...], u_ref[...])
    o32_ref[...] += jnp.dot(lhs, w_ref[...], preferred_element_type=F32)

    @pl.when(k == pl.num_programs(1) - 1)
    def _():
        g = g_ref[...]
        b = b_ref[...]

        def body(r, carry):
            rows = pl.ds(pl.multiple_of(r * LN_ROW_BLOCK, LN_ROW_BLOCK), LN_ROW_BLOCK)
            y = alpha * x32_ref[rows, :] + o32_ref[rows, :]
            _layernorm_store(y, g, b, o32_ref, ob_ref, rows)
            return carry

        lax.fori_loop(0, o32_ref.shape[0] // LN_ROW_BLOCK, body, 0)


def _outproj_ln(x32, a, u, w, g, b, layer, alpha):
    t, d = x32.shape
    ka = a.shape[-1]
    tm = min(OUT_ROWS, t)
    tk = OUT_K
    ksplit = ka // tk
    nk = w.shape[1] // tk
    single = pl.Buffered(1)
    return pl.pallas_call(
        functools.partial(_outproj_kernel, alpha=alpha, ksplit=ksplit),
        grid=(t // tm, nk),
        in_specs=[
            pl.BlockSpec((tm, d), lambda i, k: (i, 0), pipeline_mode=single),
            pl.BlockSpec((tm, tk), lambda i, k: (i, jnp.minimum(k, ksplit - 1))),
            pl.BlockSpec((tm, tk), lambda i, k: (i, jnp.maximum(k - ksplit, 0))),
            pl.BlockSpec((None, tk, d), lambda i, k: (layer, k, 0)),
            pl.BlockSpec((None, 1, d), lambda i, k: (layer, 0, 0)),
            pl.BlockSpec((None, 1, d), lambda i, k: (layer, 0, 0)),
        ],
        out_specs=[
            pl.BlockSpec((tm, d), lambda i, k: (i, 0), pipeline_mode=single),
            pl.BlockSpec((tm, d), lambda i, k: (i, 0), pipeline_mode=single),
        ],
        out_shape=[jax.ShapeDtypeStruct((t, d), F32), jax.ShapeDtypeStruct((t, d), BF16)],
        compiler_params=_params(("parallel", "arbitrary")),
        name="outproj_ln",
    )(x32, a, u, w, g, b)


def _column_layout(d):
    return {"z": 0, "cm_val": d, "cm_gate": 2 * d, "xs": 3 * d,
            "b": 4 * d, "c": 4 * d + GROUPS * STATE}


def _group_major(fwd, bwd):
    nl = fwd.shape[0]
    both = jnp.stack([fwd.reshape(nl, GROUPS, HEADS_PER_GROUP),
                      bwd.reshape(nl, GROUPS, HEADS_PER_GROUP)], axis=2)
    return both.reshape(nl, 1, 2 * GROUPS * HEADS_PER_GROUP).astype(F32)


def kernel(x_prompt, x_sample, ffn1_w_gate, ffn1_w_up, ffn1_w_down, ffn1_ln_g, ffn1_ln_b, w_in, ssd_conv_w, ssd_conv_b, dt_bias_fwd, dt_bias_bwd, a_log_fwd, a_log_bwd, d_skip, ssd_norm_w, cm_conv_w, cm_conv_b, cm_ln_g, cm_ln_b, w_out, mix_ln_g, mix_ln_b, ffn2_w_gate, ffn2_w_up, ffn2_w_down, ffn2_ln_g, ffn2_ln_b):
    depth, d, _ = w_in.shape
    alpha = float((2 * depth) ** 0.25)
    nbp, seq, _ = x_prompt.shape
    nbs = x_sample.shape[0]
    assert x_sample.shape[1] == seq and seq % CHUNK == 0
    nb = nbp + nbs
    heads = GROUPS * HEADS_PER_GROUP
    width = GROUPS * GROUP_WIDTH
    xbc = width + 2 * GROUPS * STATE
    assert d == width and w_in.shape[-1] == width + xbc + 2 * heads + 2 * d

    cast = lambda w: w.astype(BF16)
    row = lambda p: p.reshape(depth, 1, -1).astype(F32)
    f1g, f1u, f1d = cast(ffn1_w_gate), cast(ffn1_w_up), cast(ffn1_w_down)
    f2g, f2u, f2d = cast(ffn2_w_gate), cast(ffn2_w_up), cast(ffn2_w_down)
    wo = cast(w_out)
    o_xbc, o_dt, o_cm = width, width + xbc, width + xbc + 2 * heads
    w_main = cast(jnp.concatenate(
        [w_in[:, :, :o_xbc], w_in[:, :, o_cm:], w_in[:, :, o_xbc:o_dt]], axis=-1))
    perm = np.arange(2 * heads).reshape(2, GROUPS, HEADS_PER_GROUP).transpose(1, 0, 2).reshape(-1)
    w_dt = cast(w_in[:, :, o_dt:o_cm][:, :, perm])
    dtb = _group_major(dt_bias_fwd, dt_bias_bwd)
    alog = _group_major(a_log_fwd, a_log_bwd)
    dskip = jnp.repeat(d_skip.astype(F32), HEAD_DIM, axis=-1).reshape(depth, 1, width)
    col = _column_layout(d)

    x32 = jnp.concatenate([x_prompt, x_sample], axis=0).reshape(nb * seq, d)
    xb = x32.astype(BF16)
    for l in range(depth):
        x32, xb = _ffn_ln(x32, xb, f1g, f1u, f1d, row(ffn1_ln_g), row(ffn1_ln_b), l, alpha)
        proj, dt_raw = _inproj(xb, w_main, w_dt, l)
        proj3 = proj.reshape(nb, seq, -1)
        ssd_out = _ssd(proj3, dt_raw.reshape(nb, seq, -1), ssd_conv_w, row(ssd_conv_b), dtb, alog,
                       dskip, row(ssd_norm_w), l, col)
        u = _convmod(proj3, cm_conv_w, row(cm_conv_b), row(cm_ln_g), row(cm_ln_b), l, col, CM_ROWS)
        x32, xb = _outproj_ln(x32, ssd_out.reshape(nb * seq, width), u.reshape(nb * seq, d), wo,
                              row(mix_ln_g), row(mix_ln_b), l, alpha)
        x32, xb = _ffn_ln(x32, xb, f2g, f2u, f2d, row(ffn2_ln_g), row(ffn2_ln_b), l, alpha)
    y = x32.reshape(nb, seq, d)
    return (y[:nbp], y[nbp:])
```

```python
import functools

import jax
import jax.numpy as jnp
import numpy as np
from jax import lax
from jax.experimental import pallas as pl
from jax.experimental.pallas import tpu as pltpu

F32 = jnp.float32
BF16 = jnp.bfloat16

HEAD_DIM = 64
GROUPS = 8
HEADS_PER_GROUP = 8
GROUP_WIDTH = HEAD_DIM * HEADS_PER_GROUP
STATE = 128
SSD_CONV = 5
CHUNK = 128
CM_CONV = 31
LN_EPS = 1e-5
RMS_EPS = 1e-5

LANES = 128
SUBLANES = 8
VMEM_LIMIT_BYTES = 60 * 1024 * 1024

FFN_ROWS = 1024
FFN_COLS = 256
PROJ_ROWS = 1024
PROJ_COLS = 1024
OUT_ROWS = 1024
OUT_K = 512
CM_ROWS = 256
CM_HALO = 16
LN_ROW_BLOCK = 32
RES_CHUNKS = 8
RES_ROW_BLOCK = 64


def _params(sem):
    return pltpu.CompilerParams(dimension_semantics=sem, vmem_limit_bytes=VMEM_LIMIT_BYTES)


def _silu(v):
    return v * jax.nn.sigmoid(v)


def _add_residual_chunk(step, xc_ref, acc_ref, coeff):
    cw = xc_ref.shape[1]
    for c in range(RES_CHUNKS):
        @pl.when(step == c)
        def _(c=c):
            def body(r, carry):
                rows = pl.ds(pl.multiple_of(r * RES_ROW_BLOCK, RES_ROW_BLOCK), RES_ROW_BLOCK)
                acc_ref[rows, c * cw:(c + 1) * cw] += coeff * xc_ref[rows, :]
                return carry

            lax.fori_loop(0, acc_ref.shape[0] // RES_ROW_BLOCK, body, 0)


def _layernorm_rows(scale, g_ref, b_ref, o32_ref, ob_ref):
    g = g_ref[...]
    b = b_ref[...]

    def body(r, carry):
        rows = pl.ds(pl.multiple_of(r * LN_ROW_BLOCK, LN_ROW_BLOCK), LN_ROW_BLOCK)
        y = o32_ref[rows, :]
        if scale != 1.0:
            y = scale * y
        mu = jnp.mean(y, axis=-1, keepdims=True)
        d = y - mu
        var = jnp.mean(d * d, axis=-1, keepdims=True)
        out = d * lax.rsqrt(var + LN_EPS) * g + b
        o32_ref[rows, :] = out
        ob_ref[rows, :] = out.astype(BF16)
        return carry

    lax.fori_loop(0, o32_ref.shape[0] // LN_ROW_BLOCK, body, 0, unroll=2)


def _ffn_kernel(xc_ref, xb_ref, wg_ref, wu_ref, wd_ref, g_ref, b_ref, o32_ref, ob_ref, *, alpha):
    j = pl.program_id(1)

    @pl.when(j == 0)
    def _():
        o32_ref[...] = jnp.zeros_like(o32_ref)

    xb = xb_ref[...]
    gate = jnp.dot(xb, wg_ref[...], preferred_element_type=F32)
    up = jnp.dot(xb, wu_ref[...], preferred_element_type=F32)
    h = (_silu(gate) * up).astype(BF16)
    o32_ref[...] += jnp.dot(h, wd_ref[...], preferred_element_type=F32)
    _add_residual_chunk(j, xc_ref, o32_ref, 2.0 * alpha)

    @pl.when(j == pl.num_programs(1) - 1)
    def _():
        _layernorm_rows(0.5, g_ref, b_ref, o32_ref, ob_ref)


def _ffn_ln(x32, xb, wg, wu, wd, g, b, layer, alpha):
    t, d = x32.shape
    f = wg.shape[-1]
    tm = min(FFN_ROWS, t)
    tf = FFN_COLS
    nf = f // tf
    assert nf >= RES_CHUNKS and d % RES_CHUNKS == 0
    cw = d // RES_CHUNKS
    single = pl.Buffered(1)
    return pl.pallas_call(
        functools.partial(_ffn_kernel, alpha=alpha),
        grid=(t // tm, nf),
        in_specs=[
            pl.BlockSpec((tm, cw), lambda i, j: (i, jnp.minimum(j, RES_CHUNKS - 1))),
            pl.BlockSpec((tm, d), lambda i, j: (i, 0), pipeline_mode=single),
            pl.BlockSpec((None, d, tf), lambda i, j: (layer, 0, j)),
            pl.BlockSpec((None, d, tf), lambda i, j: (layer, 0, j)),
            pl.BlockSpec((None, tf, d), lambda i, j: (layer, j, 0)),
            pl.BlockSpec((None, 1, d), lambda i, j: (layer, 0, 0)),
            pl.BlockSpec((None, 1, d), lambda i, j: (layer, 0, 0)),
        ],
        out_specs=[
            pl.BlockSpec((tm, d), lambda i, j: (i, 0), pipeline_mode=single),
            pl.BlockSpec((tm, d), lambda i, j: (i, 0), pipeline_mode=single),
        ],
        out_shape=[jax.ShapeDtypeStruct((t, d), F32), jax.ShapeDtypeStruct((t, d), BF16)],
        compiler_params=_params(("parallel", "arbitrary")),
        name="ffn_ln",
    )(x32, xb, wg, wu, wd, g, b)


def _inproj_kernel(xb_ref, w_ref, wdt_ref, o_ref, odt_ref):
    xb = xb_ref[...]
    o_ref[...] = jnp.dot(xb, w_ref[...], preferred_element_type=F32)

    @pl.when(pl.program_id(1) == 0)
    def _():
        odt_ref[...] = jnp.dot(xb, wdt_ref[...], preferred_element_type=F32)


def _inproj(xb, w_main, w_dt, layer):
    t, d = xb.shape
    n = w_main.shape[-1]
    ndt = w_dt.shape[-1]
    tm = min(PROJ_ROWS, t)
    tn = PROJ_COLS
    return pl.pallas_call(
        _inproj_kernel,
        grid=(t // tm, n // tn),
        in_specs=[
            pl.BlockSpec((tm, d), lambda i, j: (i, 0)),
            pl.BlockSpec((None, d, tn), lambda i, j: (layer, 0, j)),
            pl.BlockSpec((None, d, ndt), lambda i, j: (layer, 0, 0)),
        ],
        out_specs=[
            pl.BlockSpec((tm, tn), lambda i, j: (i, j)),
            pl.BlockSpec((tm, ndt), lambda i, j: (i, 0)),
        ],
        out_shape=[jax.ShapeDtypeStruct((t, n), F32), jax.ShapeDtypeStruct((t, ndt), F32)],
        compiler_params=_params(("parallel", "arbitrary")),
        name="inproj",
    )(xb, w_main, w_dt)


SSD_PAD = 8
SSD_CONV_ROWS = 32
SSD_OUT_ROWS = 64
EXPAND_FACTORS = 3
PIECE_LANES = 64


def _expand_matrix():
    e = np.zeros((2, 2 * LANES, EXPAND_FACTORS * GROUP_WIDTH), np.float32)
    for d in range(2):
        for q in range(EXPAND_FACTORS):
            for h in range(HEADS_PER_GROUP):
                lane = 2 * HEADS_PER_GROUP * q + HEADS_PER_GROUP * d + h
                cols = slice(GROUP_WIDTH * q + HEAD_DIM * h, GROUP_WIDTH * q + HEAD_DIM * (h + 1))
                for piece in range(3):
                    e[d, PIECE_LANES * piece + lane, cols] = 1.0
    return e


def _ssd_kernel(xs_ref, bcm_ref, z_ref, dt_ref, cwx_ref, cwb_ref, cwc_ref,
                cbx_ref, cbb_ref, cbc_ref, dtb_ref, alog_ref, dskip_ref, nw_ref, e_ref,
                o_ref,
                pad_s, xs_s, bf_s, bc_s, bt_s, cc_s, acs_s, rowf_s, lhs_s, y_s, h_s):
    seq = xs_ref.shape[0]
    nchunks = seq // CHUNK
    grp = pl.program_id(1)
    half = (SSD_CONV - 1) // 2

    def conv_silu(src_ref, w, b, width, store):
        pad_s[0:SSD_PAD, 0:width] = jnp.zeros((SSD_PAD, width), F32)
        pad_s[SSD_PAD + seq:2 * SSD_PAD + seq, 0:width] = jnp.zeros((SSD_PAD, width), F32)

        def copy_body(r, carry):
            r0 = pl.multiple_of(r * CHUNK, CHUNK)
            pad_s[pl.ds(SSD_PAD + r0, CHUNK), 0:width] = src_ref[pl.ds(r0, CHUNK), :]
            return carry

        lax.fori_loop(0, seq // CHUNK, copy_body, 0)

        def conv_body(r, carry):
            r0 = pl.multiple_of(r * SSD_CONV_ROWS, SSD_CONV_ROWS)
            win = pad_s[pl.ds(r0, SSD_CONV_ROWS + 2 * SSD_PAD), 0:width]
            acc = jnp.broadcast_to(b, (SSD_CONV_ROWS, width))
            for k in range(SSD_CONV):
                off = SSD_PAD - half + k
                acc = acc + w[k:k + 1, :] * win[off:off + SSD_CONV_ROWS, :]
            store(r0, _silu(acc))
            return carry

        lax.fori_loop(0, seq // SSD_CONV_ROWS, conv_body, 0)

    def store_x(r0, v):
        xs_s[pl.ds(r0, SSD_CONV_ROWS), :] = v

    def store_bc(r0, v):
        bf_s[pl.ds(r0, SSD_CONV_ROWS), :] = v[:, 0:STATE]
        cc_s[pl.ds(r0, SSD_CONV_ROWS), :] = v[:, STATE:2 * STATE].astype(BF16)

    conv_silu(xs_ref, cwx_ref[...], cbx_ref[...], GROUP_WIDTH, store_x)
    conv_silu(bcm_ref, jnp.concatenate([cwb_ref[...], cwc_ref[...]], axis=1),
              jnp.concatenate([cbb_ref[...], cbc_ref[...]], axis=1), 2 * STATE, store_bc)

    row_id = lax.broadcasted_iota(jnp.int32, (CHUNK, LANES), 0)
    lane_id = lax.broadcasted_iota(jnp.int32, (CHUNK, LANES), 1)
    lane_shift = lax.rem(LANES - 2 * HEADS_PER_GROUP * grp, LANES)
    neg_a = -jnp.exp(alog_ref[...])
    nh2 = 2 * HEADS_PER_GROUP

    def prep_body(c, carry):
        rows = pl.ds(pl.multiple_of(c * CHUNK, CHUNK), CHUNK)
        raw = dt_ref[rows, :] + dtb_ref[...]
        dt = jnp.maximum(raw, 0.0) + jnp.log1p(jnp.exp(-jnp.abs(raw)))
        da = dt * neg_a
        dt = pltpu.roll(dt, lane_shift, 1)
        da = pltpu.roll(da, lane_shift, 1)
        fwd = da
        bwd = da
        step = 1
        while step < CHUNK:
            fwd = fwd + jnp.where(row_id >= step, pltpu.roll(fwd, step, 0), 0.0)
            bwd = bwd + jnp.where(row_id < CHUNK - step, pltpu.roll(bwd, CHUNK - step, 0), 0.0)
            step *= 2
        is_fwd = lane_id < HEADS_PER_GROUP
        acs = jnp.where(is_fwd, fwd, bwd)
        acs_s[rows, :] = acs
        rowf_s[c] = acs.T[0:nh2, :]
        last = jnp.where(is_fwd, acs[CHUNK - 1:CHUNK, :], acs[0:1, :])
        to_end = dt * jnp.exp(last - acs)
        from_start = jnp.exp(acs)
        q = jnp.where(lane_id < nh2, dt,
                      jnp.where(lane_id < 2 * nh2, pltpu.roll(to_end, nh2, 1),
                                jnp.where(lane_id < 3 * nh2, pltpu.roll(from_start, 2 * nh2, 1), 0.0)))
        hi = q.astype(BF16).astype(F32)
        rest = q - hi
        mid = rest.astype(BF16).astype(F32)
        lo = rest - mid
        lhs_s[rows, 0:LANES] = jnp.where(lane_id < PIECE_LANES, hi, pltpu.roll(mid, PIECE_LANES, 1)).astype(BF16)
        lhs_s[rows, LANES:2 * LANES] = lo.astype(BF16)
        bchunk = bf_s[rows, :]
        bc_s[rows, :] = bchunk.astype(BF16)
        bt_s[c] = bchunk.T.astype(BF16)
        return carry

    lax.fori_loop(0, nchunks, prep_body, 0, unroll=2)

    def scan_direction(backward):
        k0 = HEADS_PER_GROUP if backward else 0
        keep = (row_id <= lane_id) if backward else (row_id >= lane_id)
        edge = 0 if backward else CHUNK - 1
        h_s[...] = jnp.zeros_like(h_s)

        def chunk_body(i, carry):
            c = (nchunks - 1 - i) if backward else i
            rows = pl.ds(pl.multiple_of(c * CHUNK, CHUNK), CHUNK)
            cc = cc_s[rows, :]
            bc = bc_s[rows, :]
            scores = lax.dot_general(cc, bc, (((1,), (1,)), ((), ())), preferred_element_type=F32)
            xs_c = xs_s[rows, :]
            acs = acs_s[rows, :]
            rowf = rowf_s[c]
            wide = jnp.dot(lhs_s[rows, :], e_ref[1 if backward else 0], preferred_element_type=F32)
            dt_w = wide[:, 0:GROUP_WIDTH]
            to_end_w = wide[:, GROUP_WIDTH:2 * GROUP_WIDTH]
            from_start_w = wide[:, 2 * GROUP_WIDTH:3 * GROUP_WIDTH]
            xdt = xs_c * dt_w
            w_in = (xs_c * to_end_w).astype(BF16)
            y = jnp.dot(cc, h_s[...].astype(BF16), preferred_element_type=F32) * from_start_w
            parts = []
            for j in range(HEADS_PER_GROUP // 2):
                ms = []
                for h in (2 * j, 2 * j + 1):
                    seg = acs[:, k0 + h:k0 + h + 1] - rowf[k0 + h:k0 + h + 1, :]
                    decay = jnp.exp(jnp.where(keep, seg, -jnp.inf))
                    ms.append((scores * decay).astype(BF16))
                lhs = jnp.concatenate(ms, axis=1)
                pair = xdt[:, j * LANES:(j + 1) * LANES]
                rhs = jnp.concatenate([jnp.where(lane_id < HEAD_DIM, pair, 0.0).astype(BF16),
                                       jnp.where(lane_id >= HEAD_DIM, pair, 0.0).astype(BF16)], axis=0)
                parts.append(jnp.dot(lhs, rhs, preferred_element_type=F32))
            y = y + jnp.concatenate(parts, axis=1)
            if backward:
                y_s[rows, :] += y
            else:
                y_s[rows, :] = y
            h_s[...] = h_s[...] * from_start_w[edge:edge + 1, :] + jnp.dot(
                bt_s[c], w_in, preferred_element_type=F32)
            return carry

        lax.fori_loop(0, nchunks, chunk_body, 0, unroll=2)

    scan_direction(False)
    scan_direction(True)

    def out_body(r, carry):
        rows = pl.ds(pl.multiple_of(r * SSD_OUT_ROWS, SSD_OUT_ROWS), SSD_OUT_ROWS)
        y = y_s[rows, :] + dskip_ref[...] * xs_s[rows, :]
        gy = y * _silu(z_ref[rows, :])
        ms = jnp.mean(gy * gy, axis=-1, keepdims=True)
        o_ref[rows, :] = (gy * lax.rsqrt(ms + RMS_EPS) * nw_ref[...]).astype(BF16)
        return carry

    lax.fori_loop(0, seq // SSD_OUT_ROWS, out_body, 0, unroll=2)


def _ssd(proj3, dt3, conv_w, conv_b, dtb, alog, dskip, norm_w, expand, layer, col):
    nb, seq, _ = proj3.shape
    nchunks = seq // CHUNK
    gw, st = GROUP_WIDTH, STATE
    width = GROUPS * gw
    xs_blk, bc_blk, z_blk = col["xs"] // gw, col["bc"] // (2 * st), col["z"] // gw
    cb_blk, cc_blk = width // st, (width + GROUPS * st) // st
    return pl.pallas_call(
        _ssd_kernel,
        grid=(nb, GROUPS),
        in_specs=[
            pl.BlockSpec((None, seq, gw), lambda b, g: (b, 0, xs_blk + g)),
            pl.BlockSpec((None, seq, 2 * st), lambda b, g: (b, 0, bc_blk + g)),
            pl.BlockSpec((None, seq, gw), lambda b, g: (b, 0, z_blk + g)),
            pl.BlockSpec((None, seq, LANES), lambda b, g: (b, 0, 0)),
            pl.BlockSpec((None, SSD_CONV, gw), lambda b, g: (layer, 0, g)),
            pl.BlockSpec((None, SSD_CONV, st), lambda b, g: (layer, 0, cb_blk + g)),
            pl.BlockSpec((None, SSD_CONV, st), lambda b, g: (layer, 0, cc_blk + g)),
            pl.BlockSpec((None, 1, gw), lambda b, g: (layer, 0, g)),
            pl.BlockSpec((None, 1, st), lambda b, g: (layer, 0, cb_blk + g)),
            pl.BlockSpec((None, 1, st), lambda b, g: (layer, 0, cc_blk + g)),
            pl.BlockSpec((None, 1, LANES), lambda b, g: (layer, 0, 0)),
            pl.BlockSpec((None, 1, LANES), lambda b, g: (layer, 0, 0)),
            pl.BlockSpec((None, 1, gw), lambda b, g: (layer, 0, g)),
            pl.BlockSpec((None, 1, gw), lambda b, g: (layer, 0, g)),
            pl.BlockSpec(expand.shape, lambda b, g: (0, 0, 0)),
        ],
        out_specs=pl.BlockSpec((None, seq, gw), lambda b, g: (b, 0, g)),
        out_shape=jax.ShapeDtypeStruct((nb, seq, width), BF16),
        scratch_shapes=[
            pltpu.VMEM((seq + 2 * SSD_PAD, gw), F32),
            pltpu.VMEM((seq, gw), F32),
            pltpu.VMEM((seq, st), F32),
            pltpu.VMEM((seq, st), BF16),
            pltpu.VMEM((nchunks, st, CHUNK), BF16),
            pltpu.VMEM((seq, st), BF16),
            pltpu.VMEM((seq, LANES), F32),
            pltpu.VMEM((nchunks, 2 * HEADS_PER_GROUP, CHUNK), F32),
            pltpu.VMEM((seq, 2 * LANES), BF16),
            pltpu.VMEM((seq, gw), F32),
            pltpu.VMEM((st, gw), F32),
        ],
        compiler_params=_params(("parallel", "arbitrary")),
        name="ssd",
    )(proj3, proj3, proj3, dt3, conv_w, conv_w, conv_w, conv_b, conv_b, conv_b,
      dtb, alog, dskip, norm_w, expand)


CM_CONV_ROWS = 64
CM_CONV_LANES = 512
CM_SHIFT_ROWS = 32


def _convmod_kernel(v_ref, g_ref, vp_ref, gp_ref, vn_ref, gn_ref, w_ref, b_ref, lg_ref, lb_ref,
                    o_ref, u_s, c_s, sh_s, wb_s):
    tt, width = v_ref.shape
    t = pl.program_id(1)
    nt = pl.num_programs(1)
    half = (CM_CONV - 1) // 2

    def glu(v, g):
        return v * jax.nn.sigmoid(g)

    prev = glu(vp_ref[...], gp_ref[...])
    u_s[0:CM_HALO, :] = jnp.where(t > 0, prev, jnp.zeros_like(prev))
    nxt = glu(vn_ref[...], gn_ref[...])
    u_s[CM_HALO + tt:2 * CM_HALO + tt, :] = jnp.where(t < nt - 1, nxt, jnp.zeros_like(nxt))

    def glu_body(r, carry):
        r0 = pl.multiple_of(r * LN_ROW_BLOCK, LN_ROW_BLOCK)
        rows = pl.ds(r0, LN_ROW_BLOCK)
        u_s[pl.ds(CM_HALO + r0, LN_ROW_BLOCK), :] = glu(v_ref[rows, :], g_ref[rows, :])
        return carry

    lax.fori_loop(0, tt // LN_ROW_BLOCK, glu_body, 0)
    u_s[2 * CM_HALO + tt:2 * CM_HALO + tt + SUBLANES, :] = jnp.zeros((SUBLANES, width), F32)

    for cb in range(width // CM_CONV_LANES):
        lanes = slice(cb * CM_CONV_LANES, (cb + 1) * CM_CONV_LANES)

        def shift_body(r, carry, lanes=lanes):
            r0 = pl.multiple_of(r * CM_SHIFT_ROWS, CM_SHIFT_ROWS)
            win = u_s[pl.ds(r0, CM_SHIFT_ROWS + SUBLANES), lanes]
            for s in range(SUBLANES):
                sh_s[s, pl.ds(r0, CM_SHIFT_ROWS), :] = win[s:s + CM_SHIFT_ROWS, :]
            return carry

        lax.fori_loop(0, (tt + 2 * CM_HALO) // CM_SHIFT_ROWS, shift_body, 0)

        for k in range(CM_CONV):
            wb_s[k] = jnp.broadcast_to(w_ref[k:k + 1, lanes], (SUBLANES, CM_CONV_LANES))

        def conv_body(r, carry, lanes=lanes):
            r0 = pl.multiple_of(r * CM_CONV_ROWS, CM_CONV_ROWS)
            groups = CM_CONV_ROWS // SUBLANES
            acc = jnp.broadcast_to(b_ref[:, lanes], (groups, SUBLANES, CM_CONV_LANES))
            for k in range(CM_CONV):
                off = CM_HALO - half + k
                win = sh_s[off % SUBLANES, pl.ds(r0 + (off // SUBLANES) * SUBLANES, CM_CONV_ROWS), :]
                acc = acc + wb_s[k][None] * win.reshape(groups, SUBLANES, CM_CONV_LANES)
            c_s[pl.ds(r0, CM_CONV_ROWS), lanes] = acc.reshape(CM_CONV_ROWS, CM_CONV_LANES)
            return carry

        lax.fori_loop(0, tt // CM_CONV_ROWS, conv_body, 0)

    lg = lg_ref[...]
    lb = lb_ref[...]

    def ln_body(r, carry):
        rows = pl.ds(pl.multiple_of(r * LN_ROW_BLOCK, LN_ROW_BLOCK), LN_ROW_BLOCK)
        y = c_s[rows, :]
        mu = jnp.mean(y, axis=-1, keepdims=True)
        d = y - mu
        var = jnp.mean(d * d, axis=-1, keepdims=True)
        o_ref[rows, :] = _silu(d * lax.rsqrt(var + LN_EPS) * lg + lb).astype(BF16)
        return carry

    lax.fori_loop(0, tt // LN_ROW_BLOCK, ln_body, 0, unroll=2)


def _convmod(proj3, conv_w, conv_b, ln_g, ln_b, layer, col, rows_per_step):
    nb, seq, _ = proj3.shape
    width = conv_w.shape[-1]
    tt = min(rows_per_step, seq)
    halo_per_tile = tt // CM_HALO
    last_halo = seq // CM_HALO - 1
    v_blk, g_blk = col["cm_val"] // width, col["cm_gate"] // width

    def cur(blk):
        return pl.BlockSpec((None, tt, width), lambda b, t: (b, t, blk))

    def before(blk):
        return pl.BlockSpec((None, CM_HALO, width),
                            lambda b, t: (b, jnp.maximum(t * halo_per_tile - 1, 0), blk))

    def after(blk):
        return pl.BlockSpec((None, CM_HALO, width),
                            lambda b, t: (b, jnp.minimum((t + 1) * halo_per_tile, last_halo), blk))

    def per_layer(rows):
        return pl.BlockSpec((None, rows, width), lambda b, t: (layer, 0, 0))

    return pl.pallas_call(
        _convmod_kernel,
        grid=(nb, seq // tt),
        in_specs=[cur(v_blk), cur(g_blk), before(v_blk), before(g_blk), after(v_blk), after(g_blk),
                  per_layer(CM_CONV), per_layer(1), per_layer(1), per_layer(1)],
        out_specs=pl.BlockSpec((None, tt, width), lambda b, t: (b, t, 0)),
        out_shape=jax.ShapeDtypeStruct((nb, seq, width), BF16),
        scratch_shapes=[
            pltpu.VMEM((tt + 2 * CM_HALO + SUBLANES, width), F32),
            pltpu.VMEM((tt, width), F32),
            pltpu.VMEM((SUBLANES, tt + 2 * CM_HALO, CM_CONV_LANES), F32),
            pltpu.VMEM((CM_CONV, SUBLANES, CM_CONV_LANES), F32),
        ],
        compiler_params=_params(("parallel", "arbitrary")),
        name="convmod",
    )(proj3, proj3, proj3, proj3, proj3, proj3, conv_w, conv_b, ln_g, ln_b)


def _outproj_kernel(xc_ref, a_ref, u_ref, w_ref, g_ref, b_ref, o32_ref, ob_ref, *, alpha, ksplit):
    k = pl.program_id(1)

    @pl.when(k == 0)
    def _():
        o32_ref[...] = jnp.zeros_like(o32_ref)

    lhs = jnp.where(k < ksplit, a_ref[...], u_ref[...])
    o32_ref[...] += jnp.dot(lhs, w_ref[...], preferred_element_type=F32)
    _add_residual_chunk(k, xc_ref, o32_ref, alpha)

    @pl.when(k == pl.num_programs(1) - 1)
    def _():
        _layernorm_rows(1.0, g_ref, b_ref, o32_ref, ob_ref)


def _outproj_ln(x32, a, u, w, g, b, layer, alpha):
    t, d = x32.shape
    ka = a.shape[-1]
    tm = min(OUT_ROWS, t)
    tk = OUT_K
    ksplit = ka // tk
    nk = w.shape[1] // tk
    assert nk >= RES_CHUNKS and d % RES_CHUNKS == 0
    cw = d // RES_CHUNKS
    single = pl.Buffered(1)
    return pl.pallas_call(
        functools.partial(_outproj_kernel, alpha=alpha, ksplit=ksplit),
        grid=(t // tm, nk),
        in_specs=[
            pl.BlockSpec((tm, cw), lambda i, k: (i, jnp.minimum(k, RES_CHUNKS - 1))),
            pl.BlockSpec((tm, tk), lambda i, k: (i, jnp.minimum(k, ksplit - 1))),
            pl.BlockSpec((tm, tk), lambda i, k: (i, jnp.maximum(k - ksplit, 0))),
            pl.BlockSpec((None, tk, d), lambda i, k: (layer, k, 0)),
            pl.BlockSpec((None, 1, d), lambda i, k: (layer, 0, 0)),
            pl.BlockSpec((None, 1, d), lambda i, k: (layer, 0, 0)),
        ],
        out_specs=[
            pl.BlockSpec((tm, d), lambda i, k: (i, 0), pipeline_mode=single),
            pl.BlockSpec((tm, d), lambda i, k: (i, 0), pipeline_mode=single),
        ],
        out_shape=[jax.ShapeDtypeStruct((t, d), F32), jax.ShapeDtypeStruct((t, d), BF16)],
        compiler_params=_params(("parallel", "arbitrary")),
        name="outproj_ln",
    )(x32, a, u, w, g, b)


def _column_layout(d):
    return {"z": 0, "cm_val": d, "cm_gate": 2 * d, "xs": 3 * d, "bc": 4 * d}


def _group_major(fwd, bwd):
    nl = fwd.shape[0]
    both = jnp.stack([fwd.reshape(nl, GROUPS, HEADS_PER_GROUP),
                      bwd.reshape(nl, GROUPS, HEADS_PER_GROUP)], axis=2)
    return both.reshape(nl, 1, 2 * GROUPS * HEADS_PER_GROUP).astype(F32)


def kernel(x_prompt, x_sample, ffn1_w_gate, ffn1_w_up, ffn1_w_down, ffn1_ln_g, ffn1_ln_b, w_in, ssd_conv_w, ssd_conv_b, dt_bias_fwd, dt_bias_bwd, a_log_fwd, a_log_bwd, d_skip, ssd_norm_w, cm_conv_w, cm_conv_b, cm_ln_g, cm_ln_b, w_out, mix_ln_g, mix_ln_b, ffn2_w_gate, ffn2_w_up, ffn2_w_down, ffn2_ln_g, ffn2_ln_b):
    depth, d, _ = w_in.shape
    alpha = float((2 * depth) ** 0.25)
    nbp, seq, _ = x_prompt.shape
    nbs = x_sample.shape[0]
    assert x_sample.shape[1] == seq and seq % CHUNK == 0
    nb = nbp + nbs
    heads = GROUPS * HEADS_PER_GROUP
    width = GROUPS * GROUP_WIDTH
    bcw = GROUPS * STATE
    xbc = width + 2 * bcw
    assert d == width and w_in.shape[-1] == width + xbc + 2 * heads + 2 * d

    cast = lambda w: w.astype(BF16)
    row = lambda p: p.reshape(depth, 1, -1).astype(F32)
    f1g, f1u, f1d = cast(ffn1_w_gate), cast(ffn1_w_up), cast(ffn1_w_down)
    f2g, f2u, f2d = cast(ffn2_w_gate), cast(ffn2_w_up), cast(ffn2_w_down)
    wo = cast(w_out)
    o_xbc, o_dt, o_cm = width, width + xbc, width + xbc + 2 * heads
    o_b, o_c = o_xbc + width, o_xbc + width + bcw
    w_b = w_in[:, :, o_b:o_c].reshape(depth, d, GROUPS, STATE)
    w_c = w_in[:, :, o_c:o_dt].reshape(depth, d, GROUPS, STATE)
    w_bc = jnp.concatenate([w_b, w_c], axis=-1).reshape(depth, d, 2 * bcw)
    w_main = cast(jnp.concatenate(
        [w_in[:, :, :o_xbc], w_in[:, :, o_cm:], w_in[:, :, o_xbc:o_b], w_bc], axis=-1))
    perm = np.arange(2 * heads).reshape(2, GROUPS, HEADS_PER_GROUP).transpose(1, 0, 2).reshape(-1)
    w_dt = cast(w_in[:, :, o_dt:o_cm][:, :, perm])
    dtb = _group_major(dt_bias_fwd, dt_bias_bwd)
    alog = _group_major(a_log_fwd, a_log_bwd)
    dskip = jnp.repeat(d_skip.astype(F32), HEAD_DIM, axis=-1).reshape(depth, 1, width)
    expand = jnp.asarray(_expand_matrix(), BF16)
    col = _column_layout(d)

    x32 = jnp.concatenate([x_prompt, x_sample], axis=0).reshape(nb * seq, d)
    xb = x32.astype(BF16)
    for l in range(depth):
        x32, xb = _ffn_ln(x32, xb, f1g, f1u, f1d, row(ffn1_ln_g), row(ffn1_ln_b), l, alpha)
        proj, dt_raw = _inproj(xb, w_main, w_dt, l)
        proj3 = proj.reshape(nb, seq, -1)
        ssd_out = _ssd(proj3, dt_raw.reshape(nb, seq, -1), ssd_conv_w, row(ssd_conv_b), dtb, alog,
                       dskip, row(ssd_norm_w), expand, l, col)
        u = _convmod(proj3, cm_conv_w, row(cm_conv_b), row(cm_ln_g), row(cm_ln_b), l, col, CM_ROWS)
        x32, xb = _outproj_ln(x32, ssd_out.reshape(nb * seq, width), u.reshape(nb * seq, d), wo,
                              row(mix_ln_g), row(mix_ln_b), l, alpha)
        x32, xb = _ffn_ln(x32, xb, f2g, f2u, f2d, row(ffn2_ln_g), row(ffn2_ln_b), l, alpha)
    y = x32.reshape(nb, seq, d)
    return (y[:nbp], y[nbp:])
```

```python
import functools

import jax
import jax.numpy as jnp
import numpy as np
from jax import lax
from jax.experimental import pallas as pl
from jax.experimental.pallas import tpu as pltpu

F32 = jnp.float32
BF16 = jnp.bfloat16

HEAD_DIM = 64
GROUPS = 8
HEADS_PER_GROUP = 8
GROUP_WIDTH = HEAD_DIM * HEADS_PER_GROUP
STATE = 128
SSD_CONV = 5
CHUNK = 128
CM_CONV = 31
LN_EPS = 1e-5
RMS_EPS = 1e-5

LANES = 128
SUBLANES = 8
VMEM_LIMIT_BYTES = 60 * 1024 * 1024

FFN_ROWS = 1024
FFN_COLS = 256
PROJ_ROWS = 1024
PROJ_COLS = 1024
OUT_ROWS = 1024
OUT_K = 512
CM_ROWS = 256
CM_HALO = 16
LN_ROW_BLOCK = 16
RES_CHUNKS = 8
RES_ROW_BLOCK = 64
LN_SLABS = 8


def _params(sem):
    return pltpu.CompilerParams(dimension_semantics=sem, vmem_limit_bytes=VMEM_LIMIT_BYTES)


def _silu(v):
    return v * jax.nn.sigmoid(v)


def _add_residual_chunk(step, xc_ref, acc_ref, coeff):
    cw = xc_ref.shape[1]
    for c in range(RES_CHUNKS):
        @pl.when(step == c)
        def _(c=c):
            def body(r, carry):
                rows = pl.ds(pl.multiple_of(r * RES_ROW_BLOCK, RES_ROW_BLOCK), RES_ROW_BLOCK)
                acc_ref[rows, c * cw:(c + 1) * cw] += coeff * xc_ref[rows, :]
                return carry

            lax.fori_loop(0, acc_ref.shape[0] // RES_ROW_BLOCK, body, 0)


def _layernorm_slab(slab, scale, acc_ref, g_ref, b_ref, o32_ref, ob_ref):
    g = g_ref[...]
    b = b_ref[...]
    n = o32_ref.shape[0]
    base = slab * n

    def body(r, carry):
        r0 = pl.multiple_of(r * LN_ROW_BLOCK, LN_ROW_BLOCK)
        y = acc_ref[pl.ds(pl.multiple_of(base + r0, LN_ROW_BLOCK), LN_ROW_BLOCK), :]
        if scale != 1.0:
            y = scale * y
        mu = jnp.mean(y, axis=-1, keepdims=True)
        d = y - mu
        var = jnp.mean(d * d, axis=-1, keepdims=True)
        out = d * lax.rsqrt(var + LN_EPS) * g + b
        o32_ref[pl.ds(r0, LN_ROW_BLOCK), :] = out
        ob_ref[pl.ds(r0, LN_ROW_BLOCK), :] = out.astype(BF16)
        return carry

    lax.fori_loop(0, n // LN_ROW_BLOCK, body, 0, unroll=4)


def _ffn_kernel(xc_ref, xb_ref, wg_ref, wu_ref, wd_ref, g_ref, b_ref, o32_ref, ob_ref, acc_s, *,
                alpha, nf):
    j = pl.program_id(1)

    @pl.when(j == 0)
    def _():
        acc_s[...] = jnp.zeros_like(acc_s)

    @pl.when(j < nf)
    def _():
        xb = xb_ref[...]
        gate = jnp.dot(xb, wg_ref[...], preferred_element_type=F32)
        up = jnp.dot(xb, wu_ref[...], preferred_element_type=F32)
        h = (_silu(gate) * up).astype(BF16)
        acc_s[...] += jnp.dot(h, wd_ref[...], preferred_element_type=F32)

    _add_residual_chunk(j, xc_ref, acc_s, 2.0 * alpha)

    @pl.when(j >= nf)
    def _():
        _layernorm_slab(j - nf, 0.5, acc_s, g_ref, b_ref, o32_ref, ob_ref)


def _slab_index(i, step, first):
    return (i * LN_SLABS + jnp.clip(step - first, 0, LN_SLABS - 1), 0)


def _ffn_ln(x32, xb, wg, wu, wd, g, b, layer, alpha):
    t, d = x32.shape
    f = wg.shape[-1]
    tm = min(FFN_ROWS, t)
    tf = FFN_COLS
    nf = f // tf
    assert nf >= RES_CHUNKS and d % RES_CHUNKS == 0 and tm % (LN_SLABS * LN_ROW_BLOCK) == 0
    cw = d // RES_CHUNKS
    slab = tm // LN_SLABS
    last = nf - 1
    return pl.pallas_call(
        functools.partial(_ffn_kernel, alpha=alpha, nf=nf),
        grid=(t // tm, nf + LN_SLABS),
        in_specs=[
            pl.BlockSpec((tm, cw), lambda i, j: (i, jnp.minimum(j, RES_CHUNKS - 1))),
            pl.BlockSpec((tm, d), lambda i, j: (i, 0)),
            pl.BlockSpec((None, d, tf), lambda i, j: (layer, 0, jnp.minimum(j, last))),
            pl.BlockSpec((None, d, tf), lambda i, j: (layer, 0, jnp.minimum(j, last))),
            pl.BlockSpec((None, tf, d), lambda i, j: (layer, jnp.minimum(j, last), 0)),
            pl.BlockSpec((None, 1, d), lambda i, j: (layer, 0, 0)),
            pl.BlockSpec((None, 1, d), lambda i, j: (layer, 0, 0)),
        ],
        out_specs=[
            pl.BlockSpec((slab, d), lambda i, j: _slab_index(i, j, nf)),
            pl.BlockSpec((slab, d), lambda i, j: _slab_index(i, j, nf)),
        ],
        out_shape=[jax.ShapeDtypeStruct((t, d), F32), jax.ShapeDtypeStruct((t, d), BF16)],
        scratch_shapes=[pltpu.VMEM((tm, d), F32)],
        compiler_params=_params(("parallel", "arbitrary")),
        name="ffn_ln",
    )(x32, xb, wg, wu, wd, g, b)


def _inproj_kernel(xb_ref, w_ref, wdt_ref, o_ref, odt_ref):
    xb = xb_ref[...]
    o_ref[...] = jnp.dot(xb, w_ref[...], preferred_element_type=F32)

    @pl.when(pl.program_id(1) == 0)
    def _():
        odt_ref[...] = jnp.dot(xb, wdt_ref[...], preferred_element_type=F32)


def _inproj(xb, w_main, w_dt, layer):
    t, d = xb.shape
    n = w_main.shape[-1]
    ndt = w_dt.shape[-1]
    tm = min(PROJ_ROWS, t)
    tn = PROJ_COLS
    return pl.pallas_call(
        _inproj_kernel,
        grid=(t // tm, n // tn),
        in_specs=[
            pl.BlockSpec((tm, d), lambda i, j: (i, 0)),
            pl.BlockSpec((None, d, tn), lambda i, j: (layer, 0, j)),
            pl.BlockSpec((None, d, ndt), lambda i, j: (layer, 0, 0)),
        ],
        out_specs=[
            pl.BlockSpec((tm, tn), lambda i, j: (i, j)),
            pl.BlockSpec((tm, ndt), lambda i, j: (i, 0)),
        ],
        out_shape=[jax.ShapeDtypeStruct((t, n), F32), jax.ShapeDtypeStruct((t, ndt), F32)],
        compiler_params=_params(("parallel", "arbitrary")),
        name="inproj",
    )(xb, w_main, w_dt)


SSD_PAD = 8
SSD_CONV_ROWS = 32
SSD_OUT_ROWS = 64
EXPAND_FACTORS = 3
PIECE_LANES = 64


def _expand_matrix():
    e = np.zeros((2, 2 * LANES, EXPAND_FACTORS * GROUP_WIDTH), np.float32)
    for d in range(2):
        for q in range(EXPAND_FACTORS):
            for h in range(HEADS_PER_GROUP):
                lane = 2 * HEADS_PER_GROUP * q + HEADS_PER_GROUP * d + h
                cols = slice(GROUP_WIDTH * q + HEAD_DIM * h, GROUP_WIDTH * q + HEAD_DIM * (h + 1))
                for piece in range(3):
                    e[d, PIECE_LANES * piece + lane, cols] = 1.0
    return e


def _ssd_kernel(xs_ref, bcm_ref, z_ref, dt_ref, cwx_ref, cwb_ref, cwc_ref,
                cbx_ref, cbb_ref, cbc_ref, dtb_ref, alog_ref, dskip_ref, nw_ref, e_ref,
                o_ref,
                pad_s, xs_s, bf_s, bc_s, bt_s, cc_s, acs_s, rowf_s, lhs_s, y_s, h_s,
                dtall_s, acsall_s, toend_s, fstart_s):
    seq = xs_ref.shape[0]
    nchunks = seq // CHUNK
    grp = pl.program_id(1)
    half = (SSD_CONV - 1) // 2

    def conv_silu(src_ref, w, b, width, store):
        pad_s[0:SSD_PAD, 0:width] = jnp.zeros((SSD_PAD, width), F32)
        pad_s[SSD_PAD + seq:2 * SSD_PAD + seq, 0:width] = jnp.zeros((SSD_PAD, width), F32)

        def copy_body(r, carry):
            r0 = pl.multiple_of(r * CHUNK, CHUNK)
            pad_s[pl.ds(SSD_PAD + r0, CHUNK), 0:width] = src_ref[pl.ds(r0, CHUNK), :]
            return carry

        lax.fori_loop(0, seq // CHUNK, copy_body, 0)

        def conv_body(r, carry):
            r0 = pl.multiple_of(r * SSD_CONV_ROWS, SSD_CONV_ROWS)
            win = pad_s[pl.ds(r0, SSD_CONV_ROWS + 2 * SSD_PAD), 0:width]
            acc = jnp.broadcast_to(b, (SSD_CONV_ROWS, width))
            for k in range(SSD_CONV):
                off = SSD_PAD - half + k
                acc = acc + w[k:k + 1, :] * win[off:off + SSD_CONV_ROWS, :]
            store(r0, _silu(acc))
            return carry

        lax.fori_loop(0, seq // SSD_CONV_ROWS, conv_body, 0)

    def store_x(r0, v):
        xs_s[pl.ds(r0, SSD_CONV_ROWS), :] = v

    def store_bc(r0, v):
        bf_s[pl.ds(r0, SSD_CONV_ROWS), :] = v[:, 0:STATE]
        cc_s[pl.ds(r0, SSD_CONV_ROWS), :] = v[:, STATE:2 * STATE].astype(BF16)

    conv_silu(xs_ref, cwx_ref[...], cbx_ref[...], GROUP_WIDTH, store_x)
    conv_silu(bcm_ref, jnp.concatenate([cwb_ref[...], cwc_ref[...]], axis=1),
              jnp.concatenate([cbb_ref[...], cbc_ref[...]], axis=1), 2 * STATE, store_bc)

    row_id = lax.broadcasted_iota(jnp.int32, (CHUNK, LANES), 0)
    lane_id = lax.broadcasted_iota(jnp.int32, (CHUNK, LANES), 1)
    nh2 = 2 * HEADS_PER_GROUP

    @pl.when(grp == 0)
    def _():
        neg_a = -jnp.exp(alog_ref[...])
        is_fwd = lax.rem(lane_id, nh2) < HEADS_PER_GROUP

        def all_heads_body(c, carry):
            rows = pl.ds(pl.multiple_of(c * CHUNK, CHUNK), CHUNK)
            raw = dt_ref[rows, :] + dtb_ref[...]
            dt = jnp.maximum(raw, 0.0) + jnp.log1p(jnp.exp(-jnp.abs(raw)))
            da = dt * neg_a
            fwd = da
            bwd = da
            step = 1
            while step < CHUNK:
                fwd = fwd + jnp.where(row_id >= step, pltpu.roll(fwd, step, 0), 0.0)
                bwd = bwd + jnp.where(row_id < CHUNK - step, pltpu.roll(bwd, CHUNK - step, 0), 0.0)
                step *= 2
            acs = jnp.where(is_fwd, fwd, bwd)
            last = jnp.where(is_fwd, acs[CHUNK - 1:CHUNK, :], acs[0:1, :])
            dtall_s[rows, :] = dt
            acsall_s[rows, :] = acs
            rowf_s[c] = acs.T
            toend_s[rows, :] = dt * jnp.exp(last - acs)
            fstart_s[rows, :] = jnp.exp(acs)
            return carry

        lax.fori_loop(0, nchunks, all_heads_body, 0, unroll=2)

    shift0 = lax.rem(LANES - nh2 * grp, LANES)
    shift1 = lax.rem(LANES + nh2 - nh2 * grp, LANES)
    shift2 = lax.rem(LANES + 2 * nh2 - nh2 * grp, LANES)

    def prep_body(c, carry):
        rows = pl.ds(pl.multiple_of(c * CHUNK, CHUNK), CHUNK)
        acs = pltpu.roll(acsall_s[rows, :], shift0, 1)
        acs_s[rows, :] = acs
        q = jnp.where(lane_id < nh2, pltpu.roll(dtall_s[rows, :], shift0, 1),
                      jnp.where(lane_id < 2 * nh2, pltpu.roll(toend_s[rows, :], shift1, 1),
                                jnp.where(lane_id < 3 * nh2, pltpu.roll(fstart_s[rows, :], shift2, 1), 0.0)))
        hi = q.astype(BF16).astype(F32)
        rest = q - hi
        mid = rest.astype(BF16).astype(F32)
        lo = rest - mid
        lhs_s[rows, 0:LANES] = jnp.where(lane_id < PIECE_LANES, hi, pltpu.roll(mid, PIECE_LANES, 1)).astype(BF16)
        lhs_s[rows, LANES:2 * LANES] = lo.astype(BF16)
        bchunk = bf_s[rows, :]
        bc_s[rows, :] = bchunk.astype(BF16)
        bt_s[c] = bchunk.T.astype(BF16)
        return carry

    lax.fori_loop(0, nchunks, prep_body, 0, unroll=4)

    def scan_direction(backward):
        k0 = HEADS_PER_GROUP if backward else 0
        keep = (row_id <= lane_id) if backward else (row_id >= lane_id)
        edge = 0 if backward else CHUNK - 1
        h_s[...] = jnp.zeros_like(h_s)

        def chunk_body(i, carry):
            c = (nchunks - 1 - i) if backward else i
            rows = pl.ds(pl.multiple_of(c * CHUNK, CHUNK), CHUNK)
            cc = cc_s[rows, :]
            bc = bc_s[rows, :]
            scores = lax.dot_general(cc, bc, (((1,), (1,)), ((), ())), preferred_element_type=F32)
            xs_c = xs_s[rows, :]
            acs = acs_s[rows, :]
            rowf = rowf_s[c, pl.ds(pl.multiple_of(nh2 * grp, nh2), nh2), :]
            wide = jnp.dot(lhs_s[rows, :], e_ref[1 if backward else 0], preferred_element_type=F32)
            dt_w = wide[:, 0:GROUP_WIDTH]
            to_end_w = wide[:, GROUP_WIDTH:2 * GROUP_WIDTH]
            from_start_w = wide[:, 2 * GROUP_WIDTH:3 * GROUP_WIDTH]
            xdt = xs_c * dt_w
            w_in = (xs_c * to_end_w).astype(BF16)
            y = jnp.dot(cc, h_s[...].astype(BF16), preferred_element_type=F32) * from_start_w
            parts = []
            for j in range(HEADS_PER_GROUP // 2):
                ms = []
                for h in (2 * j, 2 * j + 1):
                    seg = acs[:, k0 + h:k0 + h + 1] - rowf[k0 + h:k0 + h + 1, :]
                    decay = jnp.exp(jnp.where(keep, seg, -jnp.inf))
                    ms.append((scores * decay).astype(BF16))
                lhs = jnp.concatenate(ms, axis=1)
                pair = xdt[:, j * LANES:(j + 1) * LANES]
                rhs = jnp.concatenate([jnp.where(lane_id < HEAD_DIM, pair, 0.0).astype(BF16),
                                       jnp.where(lane_id >= HEAD_DIM, pair, 0.0).astype(BF16)], axis=0)
                parts.append(jnp.dot(lhs, rhs, preferred_element_type=F32))
            y = y + jnp.concatenate(parts, axis=1)
            if backward:
                y_s[rows, :] += y
            else:
                y_s[rows, :] = y
            h_s[...] = h_s[...] * from_start_w[edge:edge + 1, :] + jnp.dot(
                bt_s[c], w_in, preferred_element_type=F32)
            return carry

        lax.fori_loop(0, nchunks, chunk_body, 0, unroll=2)

    scan_direction(False)
    scan_direction(True)

    def out_body(r, carry):
        rows = pl.ds(pl.multiple_of(r * SSD_OUT_ROWS, SSD_OUT_ROWS), SSD_OUT_ROWS)
        y = y_s[rows, :] + dskip_ref[...] * xs_s[rows, :]
        gy = y * _silu(z_ref[rows, :])
        ms = jnp.mean(gy * gy, axis=-1, keepdims=True)
        o_ref[rows, :] = (gy * lax.rsqrt(ms + RMS_EPS) * nw_ref[...]).astype(BF16)
        return carry

    lax.fori_loop(0, seq // SSD_OUT_ROWS, out_body, 0, unroll=2)


def _ssd(proj3, dt3, conv_w, conv_b, dtb, alog, dskip, norm_w, expand, layer, col):
    nb, seq, _ = proj3.shape
    nchunks = seq // CHUNK
    gw, st = GROUP_WIDTH, STATE
    width = GROUPS * gw
    xs_blk, bc_blk, z_blk = col["xs"] // gw, col["bc"] // (2 * st), col["z"] // gw
    cb_blk, cc_blk = width // st, (width + GROUPS * st) // st
    return pl.pallas_call(
        _ssd_kernel,
        grid=(nb, GROUPS),
        in_specs=[
            pl.BlockSpec((None, seq, gw), lambda b, g: (b, 0, xs_blk + g)),
            pl.BlockSpec((None, seq, 2 * st), lambda b, g: (b, 0, bc_blk + g)),
            pl.BlockSpec((None, seq, gw), lambda b, g: (b, 0, z_blk + g)),
            pl.BlockSpec((None, seq, LANES), lambda b, g: (b, 0, 0)),
            pl.BlockSpec((None, SSD_CONV, gw), lambda b, g: (layer, 0, g)),
            pl.BlockSpec((None, SSD_CONV, st), lambda b, g: (layer, 0, cb_blk + g)),
            pl.BlockSpec((None, SSD_CONV, st), lambda b, g: (layer, 0, cc_blk + g)),
            pl.BlockSpec((None, 1, gw), lambda b, g: (layer, 0, g)),
            pl.BlockSpec((None, 1, st), lambda b, g: (layer, 0, cb_blk + g)),
            pl.BlockSpec((None, 1, st), lambda b, g: (layer, 0, cc_blk + g)),
            pl.BlockSpec((None, 1, LANES), lambda b, g: (layer, 0, 0)),
            pl.BlockSpec((None, 1, LANES), lambda b, g: (layer, 0, 0)),
            pl.BlockSpec((None, 1, gw), lambda b, g: (layer, 0, g)),
            pl.BlockSpec((None, 1, gw), lambda b, g: (layer, 0, g)),
            pl.BlockSpec(expand.shape, lambda b, g: (0, 0, 0)),
        ],
        out_specs=pl.BlockSpec((None, seq, gw), lambda b, g: (b, 0, g)),
        out_shape=jax.ShapeDtypeStruct((nb, seq, width), BF16),
        scratch_shapes=[
            pltpu.VMEM((seq + 2 * SSD_PAD, gw), F32),
            pltpu.VMEM((seq, gw), F32),
            pltpu.VMEM((seq, st), F32),
            pltpu.VMEM((seq, st), BF16),
            pltpu.VMEM((nchunks, st, CHUNK), BF16),
            pltpu.VMEM((seq, st), BF16),
            pltpu.VMEM((seq, LANES), F32),
            pltpu.VMEM((nchunks, LANES, CHUNK), F32),
            pltpu.VMEM((seq, 2 * LANES), BF16),
            pltpu.VMEM((seq, gw), F32),
            pltpu.VMEM((st, gw), F32),
            pltpu.VMEM((seq, LANES), F32),
            pltpu.VMEM((seq, LANES), F32),
            pltpu.VMEM((seq, LANES), F32),
            pltpu.VMEM((seq, LANES), F32),
        ],
        compiler_params=_params(("parallel", "arbitrary")),
        name="ssd",
    )(proj3, proj3, proj3, dt3, conv_w, conv_w, conv_w, conv_b, conv_b, conv_b,
      dtb, alog, dskip, norm_w, expand)


CM_CONV_ROWS = 64
CM_CONV_LANES = 512
CM_SHIFT_ROWS = 32


def _convmod_kernel(v_ref, g_ref, vp_ref, gp_ref, vn_ref, gn_ref, w_ref, b_ref, lg_ref, lb_ref,
                    o_ref, u_s, c_s, sh_s, wb_s):
    tt, width = v_ref.shape
    t = pl.program_id(1)
    nt = pl.num_programs(1)
    half = (CM_CONV - 1) // 2

    def glu(v, g):
        return v * jax.nn.sigmoid(g)

    prev = glu(vp_ref[...], gp_ref[...])
    u_s[0:CM_HALO, :] = jnp.where(t > 0, prev, jnp.zeros_like(prev))
    nxt = glu(vn_ref[...], gn_ref[...])
    u_s[CM_HALO + tt:2 * CM_HALO + tt, :] = jnp.where(t < nt - 1, nxt, jnp.zeros_like(nxt))

    def glu_body(r, carry):
        r0 = pl.multiple_of(r * LN_ROW_BLOCK, LN_ROW_BLOCK)
        rows = pl.ds(r0, LN_ROW_BLOCK)
        u_s[pl.ds(CM_HALO + r0, LN_ROW_BLOCK), :] = glu(v_ref[rows, :], g_ref[rows, :])
        return carry

    lax.fori_loop(0, tt // LN_ROW_BLOCK, glu_body, 0)
    u_s[2 * CM_HALO + tt:2 * CM_HALO + tt + SUBLANES, :] = jnp.zeros((SUBLANES, width), F32)

    for cb in range(width // CM_CONV_LANES):
        lanes = slice(cb * CM_CONV_LANES, (cb + 1) * CM_CONV_LANES)

        def shift_body(r, carry, lanes=lanes):
            r0 = pl.multiple_of(r * CM_SHIFT_ROWS, CM_SHIFT_ROWS)
            win = u_s[pl.ds(r0, CM_SHIFT_ROWS + SUBLANES), lanes]
            for s in range(SUBLANES):
                sh_s[s, pl.ds(r0, CM_SHIFT_ROWS), :] = win[s:s + CM_SHIFT_ROWS, :]
            return carry

        lax.fori_loop(0, (tt + 2 * CM_HALO) // CM_SHIFT_ROWS, shift_body, 0)

        for k in range(CM_CONV):
            wb_s[k] = jnp.broadcast_to(w_ref[k:k + 1, lanes], (SUBLANES, CM_CONV_LANES))

        def conv_body(r, carry, lanes=lanes):
            r0 = pl.multiple_of(r * CM_CONV_ROWS, CM_CONV_ROWS)
            groups = CM_CONV_ROWS // SUBLANES
            acc = jnp.broadcast_to(b_ref[:, lanes], (groups, SUBLANES, CM_CONV_LANES))
            for k in range(CM_CONV):
                off = CM_HALO - half + k
                win = sh_s[off % SUBLANES, pl.ds(r0 + (off // SUBLANES) * SUBLANES, CM_CONV_ROWS), :]
                acc = acc + wb_s[k][None] * win.reshape(groups, SUBLANES, CM_CONV_LANES)
            c_s[pl.ds(r0, CM_CONV_ROWS), lanes] = acc.reshape(CM_CONV_ROWS, CM_CONV_LANES)
            return carry

        lax.fori_loop(0, tt // CM_CONV_ROWS, conv_body, 0)

    lg = lg_ref[...]
    lb = lb_ref[...]

    def ln_body(r, carry):
        rows = pl.ds(pl.multiple_of(r * LN_ROW_BLOCK, LN_ROW_BLOCK), LN_ROW_BLOCK)
        y = c_s[rows, :]
        mu = jnp.mean(y, axis=-1, keepdims=True)
        d = y - mu
        var = jnp.mean(d * d, axis=-1, keepdims=True)
        o_ref[rows, :] = _silu(d * lax.rsqrt(var + LN_EPS) * lg + lb).astype(BF16)
        return carry

    lax.fori_loop(0, tt // LN_ROW_BLOCK, ln_body, 0, unroll=4)


def _convmod(proj3, conv_w, conv_b, ln_g, ln_b, layer, col, rows_per_step):
    nb, seq, _ = proj3.shape
    width = conv_w.shape[-1]
    tt = min(rows_per_step, seq)
    halo_per_tile = tt // CM_HALO
    last_halo = seq // CM_HALO - 1
    v_blk, g_blk = col["cm_val"] // width, col["cm_gate"] // width

    def cur(blk):
        return pl.BlockSpec((None, tt, width), lambda b, t: (b, t, blk))

    def before(blk):
        return pl.BlockSpec((None, CM_HALO, width),
                            lambda b, t: (b, jnp.maximum(t * halo_per_tile - 1, 0), blk))

    def after(blk):
        return pl.BlockSpec((None, CM_HALO, width),
                            lambda b, t: (b, jnp.minimum((t + 1) * halo_per_tile, last_halo), blk))

    def per_layer(rows):
        return pl.BlockSpec((None, rows, width), lambda b, t: (layer, 0, 0))

    return pl.pallas_call(
        _convmod_kernel,
        grid=(nb, seq // tt),
        in_specs=[cur(v_blk), cur(g_blk), before(v_blk), before(g_blk), after(v_blk), after(g_blk),
                  per_layer(CM_CONV), per_layer(1), per_layer(1), per_layer(1)],
        out_specs=pl.BlockSpec((None, tt, width), lambda b, t: (b, t, 0)),
        out_shape=jax.ShapeDtypeStruct((nb, seq, width), BF16),
        scratch_shapes=[
            pltpu.VMEM((tt + 2 * CM_HALO + SUBLANES, width), F32),
            pltpu.VMEM((tt, width), F32),
            pltpu.VMEM((SUBLANES, tt + 2 * CM_HALO, CM_CONV_LANES), F32),
            pltpu.VMEM((CM_CONV, SUBLANES, CM_CONV_LANES), F32),
        ],
        compiler_params=_params(("parallel", "arbitrary")),
        name="convmod",
    )(proj3, proj3, proj3, proj3, proj3, proj3, conv_w, conv_b, ln_g, ln_b)


def _outproj_kernel(xc_ref, a_ref, u_ref, w_ref, g_ref, b_ref, o32_ref, ob_ref, acc_s, *,
                    alpha, ksplit, nk):
    k = pl.program_id(1)

    @pl.when(k == 0)
    def _():
        acc_s[...] = jnp.zeros_like(acc_s)

    @pl.when(k < nk)
    def _():
        lhs = jnp.where(k < ksplit, a_ref[...], u_ref[...])
        acc_s[...] += jnp.dot(lhs, w_ref[...], preferred_element_type=F32)

    _add_residual_chunk(k, xc_ref, acc_s, alpha)

    @pl.when(k >= nk)
    def _():
        _layernorm_slab(k - nk, 1.0, acc_s, g_ref, b_ref, o32_ref, ob_ref)


def _outproj_ln(x32, a, u, w, g, b, layer, alpha):
    t, d = x32.shape
    ka = a.shape[-1]
    tm = min(OUT_ROWS, t)
    tk = OUT_K
    ksplit = ka // tk
    nk = w.shape[1] // tk
    assert nk >= RES_CHUNKS and d % RES_CHUNKS == 0 and tm % (LN_SLABS * LN_ROW_BLOCK) == 0
    cw = d // RES_CHUNKS
    slab = tm // LN_SLABS
    return pl.pallas_call(
        functools.partial(_outproj_kernel, alpha=alpha, ksplit=ksplit, nk=nk),
        grid=(t // tm, nk + LN_SLABS),
        in_specs=[
            pl.BlockSpec((tm, cw), lambda i, k: (i, jnp.minimum(k, RES_CHUNKS - 1))),
            pl.BlockSpec((tm, tk), lambda i, k: (i, jnp.minimum(k, ksplit - 1))),
            pl.BlockSpec((tm, tk), lambda i, k: (i, jnp.clip(k - ksplit, 0, ksplit - 1))),
            pl.BlockSpec((None, tk, d), lambda i, k: (layer, jnp.minimum(k, nk - 1), 0)),
            pl.BlockSpec((None, 1, d), lambda i, k: (layer, 0, 0)),
            pl.BlockSpec((None, 1, d), lambda i, k: (layer, 0, 0)),
        ],
        out_specs=[
            pl.BlockSpec((slab, d), lambda i, k: _slab_index(i, k, nk)),
            pl.BlockSpec((slab, d), lambda i, k: _slab_index(i, k, nk)),
        ],
        out_shape=[jax.ShapeDtypeStruct((t, d), F32), jax.ShapeDtypeStruct((t, d), BF16)],
        scratch_shapes=[pltpu.VMEM((tm, d), F32)],
        compiler_params=_params(("parallel", "arbitrary")),
        name="outproj_ln",
    )(x32, a, u, w, g, b)


def _column_layout(d):
    return {"z": 0, "cm_val": d, "cm_gate": 2 * d, "xs": 3 * d, "bc": 4 * d}


def _group_major(fwd, bwd):
    nl = fwd.shape[0]
    both = jnp.stack([fwd.reshape(nl, GROUPS, HEADS_PER_GROUP),
                      bwd.reshape(nl, GROUPS, HEADS_PER_GROUP)], axis=2)
    return both.reshape(nl, 1, 2 * GROUPS * HEADS_PER_GROUP).astype(F32)


def kernel(x_prompt, x_sample, ffn1_w_gate, ffn1_w_up, ffn1_w_down, ffn1_ln_g, ffn1_ln_b, w_in, ssd_conv_w, ssd_conv_b, dt_bias_fwd, dt_bias_bwd, a_log_fwd, a_log_bwd, d_skip, ssd_norm_w, cm_conv_w, cm_conv_b, cm_ln_g, cm_ln_b, w_out, mix_ln_g, mix_ln_b, ffn2_w_gate, ffn2_w_up, ffn2_w_down, ffn2_ln_g, ffn2_ln_b):
    depth, d, _ = w_in.shape
    alpha = float((2 * depth) ** 0.25)
    nbp, seq, _ = x_prompt.shape
    nbs = x_sample.shape[0]
    assert x_sample.shape[1] == seq and seq % CHUNK == 0
    nb = nbp + nbs
    heads = GROUPS * HEADS_PER_GROUP
    width = GROUPS * GROUP_WIDTH
    bcw = GROUPS * STATE
    xbc = width + 2 * bcw
    assert d == width and w_in.shape[-1] == width + xbc + 2 * heads + 2 * d

    cast = lambda w: w.astype(BF16)
    row = lambda p: p.reshape(depth, 1, -1).astype(F32)
    f1g, f1u, f1d = cast(ffn1_w_gate), cast(ffn1_w_up), cast(ffn1_w_down)
    f2g, f2u, f2d = cast(ffn2_w_gate), cast(ffn2_w_up), cast(ffn2_w_down)
    wo = cast(w_out)
    o_xbc, o_dt, o_cm = width, width + xbc, width + xbc + 2 * heads
    o_b, o_c = o_xbc + width, o_xbc + width + bcw
    w_b = w_in[:, :, o_b:o_c].reshape(depth, d, GROUPS, STATE)
    w_c = w_in[:, :, o_c:o_dt].reshape(depth, d, GROUPS, STATE)
    w_bc = jnp.concatenate([w_b, w_c], axis=-1).reshape(depth, d, 2 * bcw)
    w_main = cast(jnp.concatenate(
        [w_in[:, :, :o_xbc], w_in[:, :, o_cm:], w_in[:, :, o_xbc:o_b], w_bc], axis=-1))
    perm = np.arange(2 * heads).reshape(2, GROUPS, HEADS_PER_GROUP).transpose(1, 0, 2).reshape(-1)
    w_dt = cast(w_in[:, :, o_dt:o_cm][:, :, perm])
    dtb = _group_major(dt_bias_fwd, dt_bias_bwd)
    alog = _group_major(a_log_fwd, a_log_bwd)
    dskip = jnp.repeat(d_skip.astype(F32), HEAD_DIM, axis=-1).reshape(depth, 1, width)
    expand = jnp.asarray(_expand_matrix(), BF16)
    col = _column_layout(d)

    def trunk(x):
        nb = x.shape[0]
        x32 = x.reshape(nb * seq, d)
        xb = x32.astype(BF16)
        for l in range(depth):
            x32, xb = _ffn_ln(x32, xb, f1g, f1u, f1d, row(ffn1_ln_g), row(ffn1_ln_b), l, alpha)
            proj, dt_raw = _inproj(xb, w_main, w_dt, l)
            proj3 = proj.reshape(nb, seq, -1)
            ssd_out = _ssd(proj3, dt_raw.reshape(nb, seq, -1), ssd_conv_w, row(ssd_conv_b), dtb, alog,
                           dskip, row(ssd_norm_w), expand, l, col)
            u = _convmod(proj3, cm_conv_w, row(cm_conv_b), row(cm_ln_g), row(cm_ln_b), l, col, CM_ROWS)
            x32, xb = _outproj_ln(x32, ssd_out.reshape(nb * seq, width), u.reshape(nb * seq, d), wo,
                                  row(mix_ln_g), row(mix_ln_b), l, alpha)
            x32, xb = _ffn_ln(x32, xb, f2g, f2u, f2d, row(ffn2_ln_g), row(ffn2_ln_b), l, alpha)
        return x32.reshape(nb, seq, d)

    return (trunk(x_prompt), trunk(x_sample))
```

```python
import functools

import jax
import jax.numpy as jnp
import numpy as np
from jax import lax
from jax.experimental import pallas as pl
from jax.experimental.pallas import tpu as pltpu

F32 = jnp.float32
BF16 = jnp.bfloat16

HEAD_DIM = 64
GROUPS = 8
HEADS_PER_GROUP = 8
GROUP_WIDTH = HEAD_DIM * HEADS_PER_GROUP
STATE = 128
SSD_CONV = 5
CHUNK = 128
CM_CONV = 31
LN_EPS = 1e-5
RMS_EPS = 1e-5

LANES = 128
SUBLANES = 8
VMEM_LIMIT_BYTES = 60 * 1024 * 1024

FFN_ROWS = 1024
FFN_COLS = 256
PROJ_ROWS = 1024
PROJ_COLS = 1024
OUT_ROWS = 1024
OUT_K = 1024
CM_ROWS = 256
CM_HALO = 16
LN_ROW_BLOCK = 16
RES_CHUNKS = 8
RES_ROW_BLOCK = 64
LN_SLABS = 8


def _params(sem):
    return pltpu.CompilerParams(dimension_semantics=sem, vmem_limit_bytes=VMEM_LIMIT_BYTES)


def _silu(v):
    return v * jax.nn.sigmoid(v)


def _add_residual_chunk(step, xc_ref, acc_ref, coeff):
    cw = xc_ref.shape[1]
    for c in range(RES_CHUNKS):
        @pl.when(step == c)
        def _(c=c):
            def body(r, carry):
                rows = pl.ds(pl.multiple_of(r * RES_ROW_BLOCK, RES_ROW_BLOCK), RES_ROW_BLOCK)
                acc_ref[rows, c * cw:(c + 1) * cw] += coeff * xc_ref[rows, :]
                return carry

            lax.fori_loop(0, acc_ref.shape[0] // RES_ROW_BLOCK, body, 0)


def _layernorm_slab(slab, scale, acc_ref, g_ref, b_ref, o32_ref, ob_ref):
    g = g_ref[...]
    b = b_ref[...]
    n = o32_ref.shape[0]
    base = slab * n

    def body(r, carry):
        r0 = pl.multiple_of(r * LN_ROW_BLOCK, LN_ROW_BLOCK)
        y = acc_ref[pl.ds(pl.multiple_of(base + r0, LN_ROW_BLOCK), LN_ROW_BLOCK), :]
        if scale != 1.0:
            y = scale * y
        mu = jnp.mean(y, axis=-1, keepdims=True)
        d = y - mu
        var = jnp.mean(d * d, axis=-1, keepdims=True)
        out = d * lax.rsqrt(var + LN_EPS) * g + b
        o32_ref[pl.ds(r0, LN_ROW_BLOCK), :] = out
        ob_ref[pl.ds(r0, LN_ROW_BLOCK), :] = out.astype(BF16)
        return carry

    lax.fori_loop(0, n // LN_ROW_BLOCK, body, 0, unroll=4)


def _ffn_kernel(xc_ref, xb_ref, wg_ref, wu_ref, wd_ref, g_ref, b_ref, o32_ref, ob_ref, acc_s, *,
                alpha, nf):
    j = pl.program_id(1)

    @pl.when(j == 0)
    def _():
        acc_s[...] = jnp.zeros_like(acc_s)

    @pl.when(j < nf)
    def _():
        xb = xb_ref[...]
        gate = jnp.dot(xb, wg_ref[...], preferred_element_type=F32)
        up = jnp.dot(xb, wu_ref[...], preferred_element_type=F32)
        h = (_silu(gate) * up).astype(BF16)
        acc_s[...] += jnp.dot(h, wd_ref[...], preferred_element_type=F32)

    _add_residual_chunk(j, xc_ref, acc_s, 2.0 * alpha)

    @pl.when(j >= nf)
    def _():
        _layernorm_slab(j - nf, 0.5, acc_s, g_ref, b_ref, o32_ref, ob_ref)


def _slab_index(i, step, first):
    return (i * LN_SLABS + jnp.clip(step - first, 0, LN_SLABS - 1), 0)


def _ffn_ln(x32, xb, wg, wu, wd, g, b, layer, alpha):
    t, d = x32.shape
    f = wg.shape[-1]
    tm = min(FFN_ROWS, t)
    tf = FFN_COLS
    nf = f // tf
    assert nf >= RES_CHUNKS and d % RES_CHUNKS == 0 and tm % (LN_SLABS * LN_ROW_BLOCK) == 0
    cw = d // RES_CHUNKS
    slab = tm // LN_SLABS
    last = nf - 1
    return pl.pallas_call(
        functools.partial(_ffn_kernel, alpha=alpha, nf=nf),
        grid=(t // tm, nf + LN_SLABS),
        in_specs=[
            pl.BlockSpec((tm, cw), lambda i, j: (i, jnp.minimum(j, RES_CHUNKS - 1))),
            pl.BlockSpec((tm, d), lambda i, j: (i, 0)),
            pl.BlockSpec((None, d, tf), lambda i, j: (layer, 0, jnp.minimum(j, last))),
            pl.BlockSpec((None, d, tf), lambda i, j: (layer, 0, jnp.minimum(j, last))),
            pl.BlockSpec((None, tf, d), lambda i, j: (layer, jnp.minimum(j, last), 0)),
            pl.BlockSpec((None, 1, d), lambda i, j: (layer, 0, 0)),
            pl.BlockSpec((None, 1, d), lambda i, j: (layer, 0, 0)),
        ],
        out_specs=[
            pl.BlockSpec((slab, d), lambda i, j: _slab_index(i, j, nf)),
            pl.BlockSpec((slab, d), lambda i, j: _slab_index(i, j, nf)),
        ],
        out_shape=[jax.ShapeDtypeStruct((t, d), F32), jax.ShapeDtypeStruct((t, d), BF16)],
        scratch_shapes=[pltpu.VMEM((tm, d), F32)],
        compiler_params=_params(("parallel", "arbitrary")),
        name="ffn_ln",
    )(x32, xb, wg, wu, wd, g, b)


def _inproj_kernel(xb_ref, w_ref, wdt_ref, o_ref, odt_ref):
    xb = xb_ref[...]
    o_ref[...] = jnp.dot(xb, w_ref[...], preferred_element_type=F32)

    @pl.when(pl.program_id(1) == 0)
    def _():
        odt_ref[...] = jnp.dot(xb, wdt_ref[...], preferred_element_type=F32)


def _inproj(xb, w_main, w_dt, layer):
    t, d = xb.shape
    n = w_main.shape[-1]
    ndt = w_dt.shape[-1]
    tm = min(PROJ_ROWS, t)
    tn = PROJ_COLS
    return pl.pallas_call(
        _inproj_kernel,
        grid=(t // tm, n // tn),
        in_specs=[
            pl.BlockSpec((tm, d), lambda i, j: (i, 0)),
            pl.BlockSpec((None, d, tn), lambda i, j: (layer, 0, j)),
            pl.BlockSpec((None, d, ndt), lambda i, j: (layer, 0, 0)),
        ],
        out_specs=[
            pl.BlockSpec((tm, tn), lambda i, j: (i, j)),
            pl.BlockSpec((tm, ndt), lambda i, j: (i, 0)),
        ],
        out_shape=[jax.ShapeDtypeStruct((t, n), F32), jax.ShapeDtypeStruct((t, ndt), F32)],
        compiler_params=_params(("parallel", "arbitrary")),
        name="inproj",
    )(xb, w_main, w_dt)


SSD_PAD = 8
SSD_CONV_ROWS = 64
SSD_OUT_ROWS = 64
EXPAND_FACTORS = 2
PIECE_LANES = 64


def _expand_matrix():
    e = np.zeros((2, 2 * LANES, EXPAND_FACTORS * GROUP_WIDTH), np.float32)
    for d in range(2):
        for q in range(EXPAND_FACTORS):
            for h in range(HEADS_PER_GROUP):
                lane = 2 * HEADS_PER_GROUP * q + HEADS_PER_GROUP * d + h
                cols = slice(GROUP_WIDTH * q + HEAD_DIM * h, GROUP_WIDTH * q + HEAD_DIM * (h + 1))
                for piece in range(3):
                    e[d, PIECE_LANES * piece + lane, cols] = 1.0
    return e


def _ssd_kernel(xs_ref, bcm_ref, z_ref, dt_ref, cwx_ref, cwb_ref, cwc_ref,
                cbx_ref, cbb_ref, cbc_ref, dtb_ref, alog_ref, dskip_ref, nw_ref, e_ref,
                o_ref,
                pad_s, xs_s, bf_s, bc_s, bt_s, cc_s, acs_s, rowf_s, lhs_s, y_s, h_s,
                acsall_s, toend_s, fstart_s):
    seq = xs_ref.shape[0]
    nchunks = seq // CHUNK
    grp = pl.program_id(1)
    half = (SSD_CONV - 1) // 2

    def conv_silu(src_ref, w, b, width, store):
        pad_s[0:SSD_PAD, 0:width] = jnp.zeros((SSD_PAD, width), F32)
        pad_s[SSD_PAD + seq:2 * SSD_PAD + seq, 0:width] = jnp.zeros((SSD_PAD, width), F32)

        def copy_body(r, carry):
            r0 = pl.multiple_of(r * CHUNK, CHUNK)
            pad_s[pl.ds(SSD_PAD + r0, CHUNK), 0:width] = src_ref[pl.ds(r0, CHUNK), :]
            return carry

        lax.fori_loop(0, seq // CHUNK, copy_body, 0)

        def conv_body(r, carry):
            r0 = pl.multiple_of(r * SSD_CONV_ROWS, SSD_CONV_ROWS)
            win = pad_s[pl.ds(r0, SSD_CONV_ROWS + 2 * SSD_PAD), 0:width]
            acc = jnp.broadcast_to(b, (SSD_CONV_ROWS, width))
            for k in range(SSD_CONV):
                off = SSD_PAD - half + k
                acc = acc + w[k:k + 1, :] * win[off:off + SSD_CONV_ROWS, :]
            store(r0, _silu(acc))
            return carry

        lax.fori_loop(0, seq // SSD_CONV_ROWS, conv_body, 0)

    def store_x(r0, v):
        xs_s[pl.ds(r0, SSD_CONV_ROWS), :] = v

    def store_bc(r0, v):
        bf_s[pl.ds(r0, SSD_CONV_ROWS), :] = v[:, 0:STATE]
        cc_s[pl.ds(r0, SSD_CONV_ROWS), :] = v[:, STATE:2 * STATE].astype(BF16)

    conv_silu(xs_ref, cwx_ref[...], cbx_ref[...], GROUP_WIDTH, store_x)
    conv_silu(bcm_ref, jnp.concatenate([cwb_ref[...], cwc_ref[...]], axis=1),
              jnp.concatenate([cbb_ref[...], cbc_ref[...]], axis=1), 2 * STATE, store_bc)

    row_id = lax.broadcasted_iota(jnp.int32, (CHUNK, LANES), 0)
    lane_id = lax.broadcasted_iota(jnp.int32, (CHUNK, LANES), 1)
    nh2 = 2 * HEADS_PER_GROUP

    @pl.when(grp == 0)
    def _():
        neg_a = -jnp.exp(alog_ref[...])
        is_fwd = lax.rem(lane_id, nh2) < HEADS_PER_GROUP

        def all_heads_body(c, carry):
            rows = pl.ds(pl.multiple_of(c * CHUNK, CHUNK), CHUNK)
            raw = dt_ref[rows, :] + dtb_ref[...]
            dt = jnp.maximum(raw, 0.0) + jnp.log1p(jnp.exp(-jnp.abs(raw)))
            da = dt * neg_a
            fwd = da
            bwd = da
            step = 1
            while step < CHUNK:
                fwd = fwd + jnp.where(row_id >= step, pltpu.roll(fwd, step, 0), 0.0)
                bwd = bwd + jnp.where(row_id < CHUNK - step, pltpu.roll(bwd, CHUNK - step, 0), 0.0)
                step *= 2
            acs = jnp.where(is_fwd, fwd, bwd)
            last = jnp.where(is_fwd, acs[CHUNK - 1:CHUNK, :], acs[0:1, :])
            acsall_s[rows, :] = acs
            rowf_s[c] = (acs - jnp.log(dt)).T
            toend_s[rows, :] = dt * jnp.exp(last - acs)
            fstart_s[rows, :] = jnp.exp(acs)
            return carry

        lax.fori_loop(0, nchunks, all_heads_body, 0, unroll=2)

    shift0 = lax.rem(LANES - nh2 * grp, LANES)
    shift1 = lax.rem(LANES + nh2 - nh2 * grp, LANES)

    def prep_body(c, carry):
        rows = pl.ds(pl.multiple_of(c * CHUNK, CHUNK), CHUNK)
        acs = pltpu.roll(acsall_s[rows, :], shift0, 1)
        acs_s[rows, :] = acs
        q = jnp.where(lane_id < nh2, pltpu.roll(toend_s[rows, :], shift0, 1),
                      jnp.where(lane_id < 2 * nh2, pltpu.roll(fstart_s[rows, :], shift1, 1), 0.0))
        hi = q.astype(BF16).astype(F32)
        rest = q - hi
        mid = rest.astype(BF16).astype(F32)
        lo = rest - mid
        lhs_s[rows, 0:LANES] = jnp.where(lane_id < PIECE_LANES, hi, pltpu.roll(mid, PIECE_LANES, 1)).astype(BF16)
        lhs_s[rows, LANES:2 * LANES] = lo.astype(BF16)
        bchunk = bf_s[rows, :]
        bc_s[rows, :] = bchunk.astype(BF16)
        bt_s[c] = bchunk.T.astype(BF16)
        return carry

    lax.fori_loop(0, nchunks, prep_body, 0, unroll=4)

    def scan_direction(backward):
        k0 = HEADS_PER_GROUP if backward else 0
        keep = (row_id <= lane_id) if backward else (row_id >= lane_id)
        edge = 0 if backward else CHUNK - 1
        h_s[...] = jnp.zeros_like(h_s)

        def chunk_body(i, carry):
            c = (nchunks - 1 - i) if backward else i
            rows = pl.ds(pl.multiple_of(c * CHUNK, CHUNK), CHUNK)
            cc = cc_s[rows, :]
            bc = bc_s[rows, :]
            scores = lax.dot_general(cc, bc, (((1,), (1,)), ((), ())), preferred_element_type=F32)
            xs_c = xs_s[rows, :]
            acs = acs_s[rows, :]
            rowf = rowf_s[c, pl.ds(pl.multiple_of(nh2 * grp, nh2), nh2), :]
            wide = jnp.dot(lhs_s[rows, :], e_ref[1 if backward else 0], preferred_element_type=F32)
            to_end_w = wide[:, 0:GROUP_WIDTH]
            from_start_w = wide[:, GROUP_WIDTH:2 * GROUP_WIDTH]
            w_in = (xs_c * to_end_w).astype(BF16)
            y = jnp.dot(cc, h_s[...].astype(BF16), preferred_element_type=F32) * from_start_w
            parts = []
            for j in range(HEADS_PER_GROUP // 2):
                ms = []
                for h in (2 * j, 2 * j + 1):
                    seg = acs[:, k0 + h:k0 + h + 1] - rowf[k0 + h:k0 + h + 1, :]
                    decay = jnp.exp(jnp.where(keep, seg, -jnp.inf))
                    ms.append((scores * decay).astype(BF16))
                lhs = jnp.concatenate(ms, axis=1)
                pair = xs_c[:, j * LANES:(j + 1) * LANES]
                rhs = jnp.concatenate([jnp.where(lane_id < HEAD_DIM, pair, 0.0).astype(BF16),
                                       jnp.where(lane_id >= HEAD_DIM, pair, 0.0).astype(BF16)], axis=0)
                parts.append(jnp.dot(lhs, rhs, preferred_element_type=F32))
            y = y + jnp.concatenate(parts, axis=1)
            if backward:
                y_s[rows, :] += y
            else:
                y_s[rows, :] = y
            h_s[...] = h_s[...] * from_start_w[edge:edge + 1, :] + jnp.dot(
                bt_s[c], w_in, preferred_element_type=F32)
            return carry

        lax.fori_loop(0, nchunks, chunk_body, 0, unroll=2)

    scan_direction(False)
    scan_direction(True)

    def out_body(r, carry):
        rows = pl.ds(pl.multiple_of(r * SSD_OUT_ROWS, SSD_OUT_ROWS), SSD_OUT_ROWS)
        y = y_s[rows, :] + dskip_ref[...] * xs_s[rows, :]
        gy = y * _silu(z_ref[rows, :])
        ms = jnp.mean(gy * gy, axis=-1, keepdims=True)
        o_ref[rows, :] = (gy * lax.rsqrt(ms + RMS_EPS) * nw_ref[...]).astype(BF16)
        return carry

    lax.fori_loop(0, seq // SSD_OUT_ROWS, out_body, 0, unroll=2)


def _ssd(proj3, dt3, conv_w, conv_b, dtb, alog, dskip, norm_w, expand, layer, col):
    nb, seq, _ = proj3.shape
    nchunks = seq // CHUNK
    gw, st = GROUP_WIDTH, STATE
    width = GROUPS * gw
    xs_blk, bc_blk, z_blk = col["xs"] // gw, col["bc"] // (2 * st), col["z"] // gw
    cb_blk, cc_blk = width // st, (width + GROUPS * st) // st
    return pl.pallas_call(
        _ssd_kernel,
        grid=(nb, GROUPS),
        in_specs=[
            pl.BlockSpec((None, seq, gw), lambda b, g: (b, 0, xs_blk + g)),
            pl.BlockSpec((None, seq, 2 * st), lambda b, g: (b, 0, bc_blk + g)),
            pl.BlockSpec((None, seq, gw), lambda b, g: (b, 0, z_blk + g)),
            pl.BlockSpec((None, seq, LANES), lambda b, g: (b, 0, 0)),
            pl.BlockSpec((None, SSD_CONV, gw), lambda b, g: (layer, 0, g)),
            pl.BlockSpec((None, SSD_CONV, st), lambda b, g: (layer, 0, cb_blk + g)),
            pl.BlockSpec((None, SSD_CONV, st), lambda b, g: (layer, 0, cc_blk + g)),
            pl.BlockSpec((None, 1, gw), lambda b, g: (layer, 0, g)),
            pl.BlockSpec((None, 1, st), lambda b, g: (layer, 0, cb_blk + g)),
            pl.BlockSpec((None, 1, st), lambda b, g: (layer, 0, cc_blk + g)),
            pl.BlockSpec((None, 1, LANES), lambda b, g: (layer, 0, 0)),
            pl.BlockSpec((None, 1, LANES), lambda b, g: (layer, 0, 0)),
            pl.BlockSpec((None, 1, gw), lambda b, g: (layer, 0, g)),
            pl.BlockSpec((None, 1, gw), lambda b, g: (layer, 0, g)),
            pl.BlockSpec(expand.shape, lambda b, g: (0, 0, 0)),
        ],
        out_specs=pl.BlockSpec((None, seq, gw), lambda b, g: (b, 0, g)),
        out_shape=jax.ShapeDtypeStruct((nb, seq, width), BF16),
        scratch_shapes=[
            pltpu.VMEM((seq + 2 * SSD_PAD, gw), F32),
            pltpu.VMEM((seq, gw), F32),
            pltpu.VMEM((seq, st), F32),
            pltpu.VMEM((seq, st), BF16),
            pltpu.VMEM((nchunks, st, CHUNK), BF16),
            pltpu.VMEM((seq, st), BF16),
            pltpu.VMEM((seq, LANES), F32),
            pltpu.VMEM((nchunks, LANES, CHUNK), F32),
            pltpu.VMEM((seq, 2 * LANES), BF16),
            pltpu.VMEM((seq, gw), F32),
            pltpu.VMEM((st, gw), F32),
            pltpu.VMEM((seq, LANES), F32),
            pltpu.VMEM((seq, LANES), F32),
            pltpu.VMEM((seq, LANES), F32),
        ],
        compiler_params=_params(("parallel", "arbitrary")),
        name="ssd",
    )(proj3, proj3, proj3, dt3, conv_w, conv_w, conv_w, conv_b, conv_b, conv_b,
      dtb, alog, dskip, norm_w, expand)


CM_CONV_ROWS = 64
CM_CONV_LANES = 512
CM_SHIFT_ROWS = 32


def _convmod_kernel(v_ref, g_ref, vp_ref, gp_ref, vn_ref, gn_ref, w_ref, b_ref, lg_ref, lb_ref,
                    o_ref, u_s, c_s, sh_s, wb_s):
    tt, width = v_ref.shape
    t = pl.program_id(1)
    nt = pl.num_programs(1)
    half = (CM_CONV - 1) // 2

    def glu(v, g):
        return v * jax.nn.sigmoid(g)

    prev = glu(vp_ref[...], gp_ref[...])
    u_s[0:CM_HALO, :] = jnp.where(t > 0, prev, jnp.zeros_like(prev))
    nxt = glu(vn_ref[...], gn_ref[...])
    u_s[CM_HALO + tt:2 * CM_HALO + tt, :] = jnp.where(t < nt - 1, nxt, jnp.zeros_like(nxt))

    def glu_body(r, carry):
        r0 = pl.multiple_of(r * LN_ROW_BLOCK, LN_ROW_BLOCK)
        rows = pl.ds(r0, LN_ROW_BLOCK)
        u_s[pl.ds(CM_HALO + r0, LN_ROW_BLOCK), :] = glu(v_ref[rows, :], g_ref[rows, :])
        return carry

    lax.fori_loop(0, tt // LN_ROW_BLOCK, glu_body, 0)
    u_s[2 * CM_HALO + tt:2 * CM_HALO + tt + SUBLANES, :] = jnp.zeros((SUBLANES, width), F32)

    for cb in range(width // CM_CONV_LANES):
        lanes = slice(cb * CM_CONV_LANES, (cb + 1) * CM_CONV_LANES)

        def shift_body(r, carry, lanes=lanes):
            r0 = pl.multiple_of(r * CM_SHIFT_ROWS, CM_SHIFT_ROWS)
            win = u_s[pl.ds(r0, CM_SHIFT_ROWS + SUBLANES), lanes]
            for s in range(SUBLANES):
                sh_s[s, pl.ds(r0, CM_SHIFT_ROWS), :] = win[s:s + CM_SHIFT_ROWS, :]
            return carry

        lax.fori_loop(0, (tt + 2 * CM_HALO) // CM_SHIFT_ROWS, shift_body, 0)

        for k in range(CM_CONV):
            wb_s[k] = jnp.broadcast_to(w_ref[k:k + 1, lanes], (SUBLANES, CM_CONV_LANES))

        def conv_body(r, carry, lanes=lanes):
            r0 = pl.multiple_of(r * CM_CONV_ROWS, CM_CONV_ROWS)
            groups = CM_CONV_ROWS // SUBLANES
            acc = jnp.broadcast_to(b_ref[:, lanes], (groups, SUBLANES, CM_CONV_LANES))
            for k in range(CM_CONV):
                off = CM_HALO - half + k
                win = sh_s[off % SUBLANES, pl.ds(r0 + (off // SUBLANES) * SUBLANES, CM_CONV_ROWS), :]
                acc = acc + wb_s[k][None] * win.reshape(groups, SUBLANES, CM_CONV_LANES)
            c_s[pl.ds(r0, CM_CONV_ROWS), lanes] = acc.reshape(CM_CONV_ROWS, CM_CONV_LANES)
            return carry

        lax.fori_loop(0, tt // CM_CONV_ROWS, conv_body, 0)

    lg = lg_ref[...]
    lb = lb_ref[...]

    def ln_body(r, carry):
        rows = pl.ds(pl.multiple_of(r * LN_ROW_BLOCK, LN_ROW_BLOCK), LN_ROW_BLOCK)
        y = c_s[rows, :]
        mu = jnp.mean(y, axis=-1, keepdims=True)
        d = y - mu
        var = jnp.mean(d * d, axis=-1, keepdims=True)
        o_ref[rows, :] = _silu(d * lax.rsqrt(var + LN_EPS) * lg + lb).astype(BF16)
        return carry

    lax.fori_loop(0, tt // LN_ROW_BLOCK, ln_body, 0, unroll=4)


def _convmod(proj3, conv_w, conv_b, ln_g, ln_b, layer, col, rows_per_step):
    nb, seq, _ = proj3.shape
    width = conv_w.shape[-1]
    tt = min(rows_per_step, seq)
    halo_per_tile = tt // CM_HALO
    last_halo = seq // CM_HALO - 1
    v_blk, g_blk = col["cm_val"] // width, col["cm_gate"] // width

    def cur(blk):
        return pl.BlockSpec((None, tt, width), lambda b, t: (b, t, blk))

    def before(blk):
        return pl.BlockSpec((None, CM_HALO, width),
                            lambda b, t: (b, jnp.maximum(t * halo_per_tile - 1, 0), blk))

    def after(blk):
        return pl.BlockSpec((None, CM_HALO, width),
                            lambda b, t: (b, jnp.minimum((t + 1) * halo_per_tile, last_halo), blk))

    def per_layer(rows):
        return pl.BlockSpec((None, rows, width), lambda b, t: (layer, 0, 0))

    return pl.pallas_call(
        _convmod_kernel,
        grid=(nb, seq // tt),
        in_specs=[cur(v_blk), cur(g_blk), before(v_blk), before(g_blk), after(v_blk), after(g_blk),
                  per_layer(CM_CONV), per_layer(1), per_layer(1), per_layer(1)],
        out_specs=pl.BlockSpec((None, tt, width), lambda b, t: (b, t, 0)),
        out_shape=jax.ShapeDtypeStruct((nb, seq, width), BF16),
        scratch_shapes=[
            pltpu.VMEM((tt + 2 * CM_HALO + SUBLANES, width), F32),
            pltpu.VMEM((tt, width), F32),
            pltpu.VMEM((SUBLANES, tt + 2 * CM_HALO, CM_CONV_LANES), F32),
            pltpu.VMEM((CM_CONV, SUBLANES, CM_CONV_LANES), F32),
        ],
        compiler_params=_params(("parallel", "arbitrary")),
        name="convmod",
    )(proj3, proj3, proj3, proj3, proj3, proj3, conv_w, conv_b, ln_g, ln_b)


def _outproj_kernel(xc_ref, a_ref, u_ref, w_ref, g_ref, b_ref, o32_ref, ob_ref, acc_s, *,
                    alpha, ksplit, nk):
    k = pl.program_id(1)

    @pl.when(k == 0)
    def _():
        acc_s[...] = jnp.zeros_like(acc_s)

    @pl.when(k < nk)
    def _():
        lhs = jnp.where(k < ksplit, a_ref[...], u_ref[...])
        acc_s[...] += jnp.dot(lhs, w_ref[...], preferred_element_type=F32)

    _add_residual_chunk(k, xc_ref, acc_s, alpha)

    @pl.when(k >= nk)
    def _():
        _layernorm_slab(k - nk, 1.0, acc_s, g_ref, b_ref, o32_ref, ob_ref)


def _outproj_ln(x32, a, u, w, g, b, layer, alpha):
    t, d = x32.shape
    ka = a.shape[-1]
    tm = min(OUT_ROWS, t)
    tk = OUT_K
    ksplit = ka // tk
    nk = w.shape[1] // tk
    assert nk >= RES_CHUNKS and d % RES_CHUNKS == 0 and tm % (LN_SLABS * LN_ROW_BLOCK) == 0
    cw = d // RES_CHUNKS
    slab = tm // LN_SLABS
    return pl.pallas_call(
        functools.partial(_outproj_kernel, alpha=alpha, ksplit=ksplit, nk=nk),
        grid=(t // tm, nk + LN_SLABS),
        in_specs=[
            pl.BlockSpec((tm, cw), lambda i, k: (i, jnp.minimum(k, RES_CHUNKS - 1))),
            pl.BlockSpec((tm, tk), lambda i, k: (i, jnp.minimum(k, ksplit - 1))),
            pl.BlockSpec((tm, tk), lambda i, k: (i, jnp.clip(k - ksplit, 0, ksplit - 1))),
            pl.BlockSpec((None, tk, d), lambda i, k: (layer, jnp.minimum(k, nk - 1), 0)),
            pl.BlockSpec((None, 1, d), lambda i, k: (layer, 0, 0)),
            pl.BlockSpec((None, 1, d), lambda i, k: (layer, 0, 0)),
        ],
        out_specs=[
            pl.BlockSpec((slab, d), lambda i, k: _slab_index(i, k, nk)),
            pl.BlockSpec((slab, d), lambda i, k: _slab_index(i, k, nk)),
        ],
        out_shape=[jax.ShapeDtypeStruct((t, d), F32), jax.ShapeDtypeStruct((t, d), BF16)],
        scratch_shapes=[pltpu.VMEM((tm, d), F32)],
        compiler_params=_params(("parallel", "arbitrary")),
        name="outproj_ln",
    )(x32, a, u, w, g, b)


def _column_layout(d):
    return {"z": 0, "cm_val": d, "cm_gate": 2 * d, "xs": 3 * d, "bc": 4 * d}


def _group_major(fwd, bwd):
    nl = fwd.shape[0]
    both = jnp.stack([fwd.reshape(nl, GROUPS, HEADS_PER_GROUP),
                      bwd.reshape(nl, GROUPS, HEADS_PER_GROUP)], axis=2)
    return both.reshape(nl, 1, 2 * GROUPS * HEADS_PER_GROUP).astype(F32)


def kernel(x_prompt, x_sample, ffn1_w_gate, ffn1_w_up, ffn1_w_down, ffn1_ln_g, ffn1_ln_b, w_in, ssd_conv_w, ssd_conv_b, dt_bias_fwd, dt_bias_bwd, a_log_fwd, a_log_bwd, d_skip, ssd_norm_w, cm_conv_w, cm_conv_b, cm_ln_g, cm_ln_b, w_out, mix_ln_g, mix_ln_b, ffn2_w_gate, ffn2_w_up, ffn2_w_down, ffn2_ln_g, ffn2_ln_b):
    depth, d, _ = w_in.shape
    alpha = float((2 * depth) ** 0.25)
    nbp, seq, _ = x_prompt.shape
    nbs = x_sample.shape[0]
    assert x_sample.shape[1] == seq and seq % CHUNK == 0
    nb = nbp + nbs
    heads = GROUPS * HEADS_PER_GROUP
    width = GROUPS * GROUP_WIDTH
    bcw = GROUPS * STATE
    xbc = width + 2 * bcw
    assert d == width and w_in.shape[-1] == width + xbc + 2 * heads + 2 * d

    cast = lambda w: w.astype(BF16)
    row = lambda p: p.reshape(depth, 1, -1).astype(F32)
    f1g, f1u, f1d = cast(ffn1_w_gate), cast(ffn1_w_up), cast(ffn1_w_down)
    f2g, f2u, f2d = cast(ffn2_w_gate), cast(ffn2_w_up), cast(ffn2_w_down)
    wo = cast(w_out)
    o_xbc, o_dt, o_cm = width, width + xbc, width + xbc + 2 * heads
    o_b, o_c = o_xbc + width, o_xbc + width + bcw
    w_in_b = cast(w_in)
    w_b = w_in_b[:, :, o_b:o_c].reshape(depth, d, GROUPS, STATE)
    w_c = w_in_b[:, :, o_c:o_dt].reshape(depth, d, GROUPS, STATE)
    w_bc = jnp.concatenate([w_b, w_c], axis=-1).reshape(depth, d, 2 * bcw)
    w_main = jnp.concatenate(
        [w_in_b[:, :, :o_xbc], w_in_b[:, :, o_cm:], w_in_b[:, :, o_xbc:o_b], w_bc], axis=-1)
    perm = np.arange(2 * heads).reshape(2, GROUPS, HEADS_PER_GROUP).transpose(1, 0, 2).reshape(-1)
    w_dt = w_in_b[:, :, o_dt:o_cm][:, :, perm]
    dtb = _group_major(dt_bias_fwd, dt_bias_bwd)
    alog = _group_major(a_log_fwd, a_log_bwd)
    dskip = jnp.repeat(d_skip.astype(F32), HEAD_DIM, axis=-1).reshape(depth, 1, width)
    expand = jnp.asarray(_expand_matrix(), BF16)
    col = _column_layout(d)

    def trunk(x):
        nb = x.shape[0]
        x32 = x.reshape(nb * seq, d)
        xb = x32.astype(BF16)
        for l in range(depth):
            x32, xb = _ffn_ln(x32, xb, f1g, f1u, f1d, row(ffn1_ln_g), row(ffn1_ln_b), l, alpha)
            proj, dt_raw = _inproj(xb, w_main, w_dt, l)
            proj3 = proj.reshape(nb, seq, -1)
            ssd_out = _ssd(proj3, dt_raw.reshape(nb, seq, -1), ssd_conv_w, row(ssd_conv_b), dtb, alog,
                           dskip, row(ssd_norm_w), expand, l, col)
            u = _convmod(proj3, cm_conv_w, row(cm_conv_b), row(cm_ln_g), row(cm_ln_b), l, col, CM_ROWS)
            x32, xb = _outproj_ln(x32, ssd_out.reshape(nb * seq, width), u.reshape(nb * seq, d), wo,
                                  row(mix_ln_g), row(mix_ln_b), l, alpha)
            x32, xb = _ffn_ln(x32, xb, f2g, f2u, f2d, row(ffn2_ln_g), row(ffn2_ln_b), l, alpha)
        return x32.reshape(nb, seq, d)

    return (trunk(x_prompt), trunk(x_sample))
```

```python
import functools

import jax
import jax.numpy as jnp
import numpy as np
from jax import lax
from jax.experimental import pallas as pl
from jax.experimental.pallas import tpu as pltpu

F32 = jnp.float32
BF16 = jnp.bfloat16

HEAD_DIM = 64
GROUPS = 8
HEADS_PER_GROUP = 8
GROUP_WIDTH = HEAD_DIM * HEADS_PER_GROUP
STATE = 128
SSD_CONV = 5
CHUNK = 128
CM_CONV = 31
LN_EPS = 1e-5
RMS_EPS = 1e-5

LANES = 128
SUBLANES = 8
VMEM_LIMIT_BYTES = 60 * 1024 * 1024

FFN_ROWS = 1024
FFN_COLS = 256
PROJ_ROWS = 1024
PROJ_COLS = 1024
OUT_ROWS = 1024
OUT_K = 1024
CM_ROWS = 256
CM_HALO = 16
LN_ROW_BLOCK = 16
RES_CHUNKS = 8
RES_ROW_BLOCK = 64
LN_SLABS = 8


def _params(sem):
    return pltpu.CompilerParams(dimension_semantics=sem, vmem_limit_bytes=VMEM_LIMIT_BYTES)


def _silu(v):
    return v * jax.nn.sigmoid(v)


def _add_residual_chunk(step, xc_ref, acc_ref, coeff):
    cw = xc_ref.shape[1]
    for c in range(RES_CHUNKS):
        @pl.when(step == c)
        def _(c=c):
            def body(r, carry):
                rows = pl.ds(pl.multiple_of(r * RES_ROW_BLOCK, RES_ROW_BLOCK), RES_ROW_BLOCK)
                acc_ref[rows, c * cw:(c + 1) * cw] += coeff * xc_ref[rows, :]
                return carry

            lax.fori_loop(0, acc_ref.shape[0] // RES_ROW_BLOCK, body, 0)


def _layernorm_slab(slab, scale, acc_ref, g_ref, b_ref, o32_ref, ob_ref):
    g = g_ref[...]
    b = b_ref[...]
    n = o32_ref.shape[0]
    base = slab * n

    def body(r, carry):
        r0 = pl.multiple_of(r * LN_ROW_BLOCK, LN_ROW_BLOCK)
        y = acc_ref[pl.ds(pl.multiple_of(base + r0, LN_ROW_BLOCK), LN_ROW_BLOCK), :]
        if scale != 1.0:
            y = scale * y
        mu = jnp.mean(y, axis=-1, keepdims=True)
        d = y - mu
        var = jnp.mean(d * d, axis=-1, keepdims=True)
        out = d * lax.rsqrt(var + LN_EPS) * g + b
        o32_ref[pl.ds(r0, LN_ROW_BLOCK), :] = out
        ob_ref[pl.ds(r0, LN_ROW_BLOCK), :] = out.astype(BF16)
        return carry

    lax.fori_loop(0, n // LN_ROW_BLOCK, body, 0, unroll=8)


def _ffn_kernel(xc_ref, xb_ref, wg_ref, wu_ref, wd_ref, g_ref, b_ref, o32_ref, ob_ref, acc_s, *,
                alpha, nf):
    j = pl.program_id(1)

    @pl.when(j == 0)
    def _():
        acc_s[...] = jnp.zeros_like(acc_s)

    @pl.when(j < nf)
    def _():
        xb = xb_ref[...]
        gate = jnp.dot(xb, wg_ref[...], preferred_element_type=F32)
        up = jnp.dot(xb, wu_ref[...], preferred_element_type=F32)
        h = (_silu(gate) * up).astype(BF16)
        acc_s[...] += jnp.dot(h, wd_ref[...], preferred_element_type=F32)

    _add_residual_chunk(j, xc_ref, acc_s, 2.0 * alpha)

    @pl.when(j >= nf)
    def _():
        _layernorm_slab(j - nf, 0.5, acc_s, g_ref, b_ref, o32_ref, ob_ref)


def _slab_index(i, step, first):
    return (i * LN_SLABS + jnp.clip(step - first, 0, LN_SLABS - 1), 0)


def _ffn_ln(x32, xb, wg, wu, wd, g, b, layer, alpha):
    t, d = x32.shape
    f = wg.shape[-1]
    tm = min(FFN_ROWS, t)
    tf = FFN_COLS
    nf = f // tf
    assert nf >= RES_CHUNKS and d % RES_CHUNKS == 0 and tm % (LN_SLABS * LN_ROW_BLOCK) == 0
    cw = d // RES_CHUNKS
    slab = tm // LN_SLABS
    last = nf - 1
    return pl.pallas_call(
        functools.partial(_ffn_kernel, alpha=alpha, nf=nf),
        grid=(t // tm, nf + LN_SLABS),
        in_specs=[
            pl.BlockSpec((tm, cw), lambda i, j: (i, jnp.minimum(j, RES_CHUNKS - 1))),
            pl.BlockSpec((tm, d), lambda i, j: (i, 0)),
            pl.BlockSpec((None, d, tf), lambda i, j: (layer, 0, jnp.minimum(j, last))),
            pl.BlockSpec((None, d, tf), lambda i, j: (layer, 0, jnp.minimum(j, last))),
            pl.BlockSpec((None, tf, d), lambda i, j: (layer, jnp.minimum(j, last), 0)),
            pl.BlockSpec((None, 1, d), lambda i, j: (layer, 0, 0)),
            pl.BlockSpec((None, 1, d), lambda i, j: (layer, 0, 0)),
        ],
        out_specs=[
            pl.BlockSpec((slab, d), lambda i, j: _slab_index(i, j, nf)),
            pl.BlockSpec((slab, d), lambda i, j: _slab_index(i, j, nf)),
        ],
        out_shape=[jax.ShapeDtypeStruct((t, d), F32), jax.ShapeDtypeStruct((t, d), BF16)],
        scratch_shapes=[pltpu.VMEM((tm, d), F32)],
        compiler_params=_params(("parallel", "arbitrary")),
        name="ffn_ln",
    )(x32, xb, wg, wu, wd, g, b)


def _inproj_kernel(xb_ref, w_ref, wdt_ref, o_ref, odt_ref):
    xb = xb_ref[...]
    o_ref[...] = jnp.dot(xb, w_ref[...], preferred_element_type=F32)

    @pl.when(pl.program_id(1) == 0)
    def _():
        odt_ref[...] = jnp.dot(xb, wdt_ref[...], preferred_element_type=F32)


def _inproj(xb, w_main, w_dt, layer):
    t, d = xb.shape
    n = w_main.shape[-1]
    ndt = w_dt.shape[-1]
    tm = min(PROJ_ROWS, t)
    tn = PROJ_COLS
    return pl.pallas_call(
        _inproj_kernel,
        grid=(t // tm, n // tn),
        in_specs=[
            pl.BlockSpec((tm, d), lambda i, j: (i, 0)),
            pl.BlockSpec((None, d, tn), lambda i, j: (layer, 0, j)),
            pl.BlockSpec((None, d, ndt), lambda i, j: (layer, 0, 0)),
        ],
        out_specs=[
            pl.BlockSpec((tm, tn), lambda i, j: (i, j)),
            pl.BlockSpec((tm, ndt), lambda i, j: (i, 0)),
        ],
        out_shape=[jax.ShapeDtypeStruct((t, n), F32), jax.ShapeDtypeStruct((t, ndt), F32)],
        compiler_params=_params(("parallel", "arbitrary")),
        name="inproj",
    )(xb, w_main, w_dt)


SSD_PAD = 8
SSD_CONV_ROWS = 64
SSD_OUT_ROWS = 64
EXPAND_FACTORS = 2
PIECE_LANES = 64


def _expand_matrix():
    e = np.zeros((2, 2 * LANES, EXPAND_FACTORS * GROUP_WIDTH), np.float32)
    for d in range(2):
        for q in range(EXPAND_FACTORS):
            for h in range(HEADS_PER_GROUP):
                lane = 2 * HEADS_PER_GROUP * q + HEADS_PER_GROUP * d + h
                cols = slice(GROUP_WIDTH * q + HEAD_DIM * h, GROUP_WIDTH * q + HEAD_DIM * (h + 1))
                for piece in range(3):
                    e[d, PIECE_LANES * piece + lane, cols] = 1.0
    return e


def _ssd_kernel(xs_ref, bcm_ref, z_ref, dt_ref, cwx_ref, cwb_ref, cwc_ref,
                cbx_ref, cbb_ref, cbc_ref, dtb_ref, alog_ref, dskip_ref, nw_ref, e_ref,
                o_ref,
                pad_s, xs_s, bf_s, bc_s, bt_s, cc_s, acs_s, rowf_s, lhs_s, y_s, h_s,
                acsall_s, toend_s, fstart_s):
    seq = xs_ref.shape[0]
    nchunks = seq // CHUNK
    grp = pl.program_id(1)
    half = (SSD_CONV - 1) // 2

    def conv_silu(src_ref, w, b, width, store):
        pad_s[0:SSD_PAD, 0:width] = jnp.zeros((SSD_PAD, width), F32)
        pad_s[SSD_PAD + seq:2 * SSD_PAD + seq, 0:width] = jnp.zeros((SSD_PAD, width), F32)

        def copy_body(r, carry):
            r0 = pl.multiple_of(r * CHUNK, CHUNK)
            pad_s[pl.ds(SSD_PAD + r0, CHUNK), 0:width] = src_ref[pl.ds(r0, CHUNK), :]
            return carry

        lax.fori_loop(0, seq // CHUNK, copy_body, 0)

        def conv_body(r, carry):
            r0 = pl.multiple_of(r * SSD_CONV_ROWS, SSD_CONV_ROWS)
            win = pad_s[pl.ds(r0, SSD_CONV_ROWS + 2 * SSD_PAD), 0:width]
            acc = jnp.broadcast_to(b, (SSD_CONV_ROWS, width))
            for k in range(SSD_CONV):
                off = SSD_PAD - half + k
                acc = acc + w[k:k + 1, :] * win[off:off + SSD_CONV_ROWS, :]
            store(r0, _silu(acc))
            return carry

        lax.fori_loop(0, seq // SSD_CONV_ROWS, conv_body, 0)

    def store_x(r0, v):
        xs_s[pl.ds(r0, SSD_CONV_ROWS), :] = v

    def store_bc(r0, v):
        bf_s[pl.ds(r0, SSD_CONV_ROWS), :] = v[:, 0:STATE]
        cc_s[pl.ds(r0, SSD_CONV_ROWS), :] = v[:, STATE:2 * STATE].astype(BF16)

    conv_silu(xs_ref, cwx_ref[...], cbx_ref[...], GROUP_WIDTH, store_x)
    conv_silu(bcm_ref, jnp.concatenate([cwb_ref[...], cwc_ref[...]], axis=1),
              jnp.concatenate([cbb_ref[...], cbc_ref[...]], axis=1), 2 * STATE, store_bc)

    row_id = lax.broadcasted_iota(jnp.int32, (CHUNK, LANES), 0)
    lane_id = lax.broadcasted_iota(jnp.int32, (CHUNK, LANES), 1)
    nh2 = 2 * HEADS_PER_GROUP

    @pl.when(grp == 0)
    def _():
        neg_a = -jnp.exp(alog_ref[...])
        is_fwd = lax.rem(lane_id, nh2) < HEADS_PER_GROUP

        def all_heads_body(c, carry):
            rows = pl.ds(pl.multiple_of(c * CHUNK, CHUNK), CHUNK)
            raw = dt_ref[rows, :] + dtb_ref[...]
            dt = jnp.maximum(raw, 0.0) + jnp.log1p(jnp.exp(-jnp.abs(raw)))
            da = dt * neg_a
            fwd = da
            bwd = da
            step = 1
            while step < CHUNK:
                fwd = fwd + jnp.where(row_id >= step, pltpu.roll(fwd, step, 0), 0.0)
                bwd = bwd + jnp.where(row_id < CHUNK - step, pltpu.roll(bwd, CHUNK - step, 0), 0.0)
                step *= 2
            acs = jnp.where(is_fwd, fwd, bwd)
            last = jnp.where(is_fwd, acs[CHUNK - 1:CHUNK, :], acs[0:1, :])
            acsall_s[rows, :] = acs
            rowf_s[c] = (acs - jnp.log(dt)).T
            toend_s[rows, :] = dt * jnp.exp(last - acs)
            fstart_s[rows, :] = jnp.exp(acs)
            return carry

        lax.fori_loop(0, nchunks, all_heads_body, 0, unroll=2)

    shift0 = lax.rem(LANES - nh2 * grp, LANES)
    shift1 = lax.rem(LANES + nh2 - nh2 * grp, LANES)

    def prep_body(c, carry):
        rows = pl.ds(pl.multiple_of(c * CHUNK, CHUNK), CHUNK)
        acs = pltpu.roll(acsall_s[rows, :], shift0, 1)
        acs_s[rows, :] = acs
        q = jnp.where(lane_id < nh2, pltpu.roll(toend_s[rows, :], shift0, 1),
                      jnp.where(lane_id < 2 * nh2, pltpu.roll(fstart_s[rows, :], shift1, 1), 0.0))
        hi = q.astype(BF16).astype(F32)
        rest = q - hi
        mid = rest.astype(BF16).astype(F32)
        lo = rest - mid
        lhs_s[rows, 0:LANES] = jnp.where(lane_id < PIECE_LANES, hi, pltpu.roll(mid, PIECE_LANES, 1)).astype(BF16)
        lhs_s[rows, LANES:2 * LANES] = lo.astype(BF16)
        bchunk = bf_s[rows, :]
        bc_s[rows, :] = bchunk.astype(BF16)
        bt_s[c] = bchunk.T.astype(BF16)
        return carry

    lax.fori_loop(0, nchunks, prep_body, 0, unroll=4)

    def scan_direction(backward):
        k0 = HEADS_PER_GROUP if backward else 0
        keep = (row_id <= lane_id) if backward else (row_id >= lane_id)
        edge = 0 if backward else CHUNK - 1
        h_s[...] = jnp.zeros_like(h_s)

        def chunk_body(i, carry):
            c = (nchunks - 1 - i) if backward else i
            rows = pl.ds(pl.multiple_of(c * CHUNK, CHUNK), CHUNK)
            cc = cc_s[rows, :]
            bc = bc_s[rows, :]
            scores = lax.dot_general(cc, bc, (((1,), (1,)), ((), ())), preferred_element_type=F32)
            xs_c = xs_s[rows, :]
            acs = acs_s[rows, :]
            rowf = rowf_s[c, pl.ds(pl.multiple_of(nh2 * grp, nh2), nh2), :]
            wide = jnp.dot(lhs_s[rows, :], e_ref[1 if backward else 0], preferred_element_type=F32)
            to_end_w = wide[:, 0:GROUP_WIDTH]
            from_start_w = wide[:, GROUP_WIDTH:2 * GROUP_WIDTH]
            w_in = (xs_c * to_end_w).astype(BF16)
            y = jnp.dot(cc, h_s[...].astype(BF16), preferred_element_type=F32) * from_start_w
            parts = []
            for j in range(HEADS_PER_GROUP // 2):
                ms = []
                for h in (2 * j, 2 * j + 1):
                    seg = acs[:, k0 + h:k0 + h + 1] - rowf[k0 + h:k0 + h + 1, :]
                    decay = jnp.exp(jnp.where(keep, seg, -jnp.inf))
                    ms.append((scores * decay).astype(BF16))
                lhs = jnp.concatenate(ms, axis=1)
                pair = xs_c[:, j * LANES:(j + 1) * LANES]
                rhs = jnp.concatenate([jnp.where(lane_id < HEAD_DIM, pair, 0.0).astype(BF16),
                                       jnp.where(lane_id >= HEAD_DIM, pair, 0.0).astype(BF16)], axis=0)
                parts.append(jnp.dot(lhs, rhs, preferred_element_type=F32))
            y = y + jnp.concatenate(parts, axis=1)
            if backward:
                y_s[rows, :] += y
            else:
                y_s[rows, :] = y
            h_s[...] = h_s[...] * from_start_w[edge:edge + 1, :] + jnp.dot(
                bt_s[c], w_in, preferred_element_type=F32)
            return carry

        lax.fori_loop(0, nchunks, chunk_body, 0, unroll=4)

    scan_direction(False)
    scan_direction(True)

    def out_body(r, carry):
        rows = pl.ds(pl.multiple_of(r * SSD_OUT_ROWS, SSD_OUT_ROWS), SSD_OUT_ROWS)
        y = y_s[rows, :] + dskip_ref[...] * xs_s[rows, :]
        gy = y * _silu(z_ref[rows, :])
        ms = jnp.mean(gy * gy, axis=-1, keepdims=True)
        o_ref[rows, :] = (gy * lax.rsqrt(ms + RMS_EPS) * nw_ref[...]).astype(BF16)
        return carry

    lax.fori_loop(0, seq // SSD_OUT_ROWS, out_body, 0, unroll=2)


def _ssd(proj3, dt3, conv_w, conv_b, dtb, alog, dskip, norm_w, expand, layer, col):
    nb, seq, _ = proj3.shape
    nchunks = seq // CHUNK
    gw, st = GROUP_WIDTH, STATE
    width = GROUPS * gw
    xs_blk, bc_blk, z_blk = col["xs"] // gw, col["bc"] // (2 * st), col["z"] // gw
    cb_blk, cc_blk = width // st, (width + GROUPS * st) // st
    return pl.pallas_call(
        _ssd_kernel,
        grid=(nb, GROUPS),
        in_specs=[
            pl.BlockSpec((None, seq, gw), lambda b, g: (b, 0, xs_blk + g)),
            pl.BlockSpec((None, seq, 2 * st), lambda b, g: (b, 0, bc_blk + g)),
            pl.BlockSpec((None, seq, gw), lambda b, g: (b, 0, z_blk + g)),
            pl.BlockSpec((None, seq, LANES), lambda b, g: (b, 0, 0)),
            pl.BlockSpec((None, SSD_CONV, gw), lambda b, g: (layer, 0, g)),
            pl.BlockSpec((None, SSD_CONV, st), lambda b, g: (layer, 0, cb_blk + g)),
            pl.BlockSpec((None, SSD_CONV, st), lambda b, g: (layer, 0, cc_blk + g)),
            pl.BlockSpec((None, 1, gw), lambda b, g: (layer, 0, g)),
            pl.BlockSpec((None, 1, st), lambda b, g: (layer, 0, cb_blk + g)),
            pl.BlockSpec((None, 1, st), lambda b, g: (layer, 0, cc_blk + g)),
            pl.BlockSpec((None, 1, LANES), lambda b, g: (layer, 0, 0)),
            pl.BlockSpec((None, 1, LANES), lambda b, g: (layer, 0, 0)),
            pl.BlockSpec((None, 1, gw), lambda b, g: (layer, 0, g)),
            pl.BlockSpec((None, 1, gw), lambda b, g: (layer, 0, g)),
            pl.BlockSpec(expand.shape, lambda b, g: (0, 0, 0)),
        ],
        out_specs=pl.BlockSpec((None, seq, gw), lambda b, g: (b, 0, g)),
        out_shape=jax.ShapeDtypeStruct((nb, seq, width), BF16),
        scratch_shapes=[
            pltpu.VMEM((seq + 2 * SSD_PAD, gw), F32),
            pltpu.VMEM((seq, gw), F32),
            pltpu.VMEM((seq, st), F32),
            pltpu.VMEM((seq, st), BF16),
            pltpu.VMEM((nchunks, st, CHUNK), BF16),
            pltpu.VMEM((seq, st), BF16),
            pltpu.VMEM((seq, LANES), F32),
            pltpu.VMEM((nchunks, LANES, CHUNK), F32),
            pltpu.VMEM((seq, 2 * LANES), BF16),
            pltpu.VMEM((seq, gw), F32),
            pltpu.VMEM((st, gw), F32),
            pltpu.VMEM((seq, LANES), F32),
            pltpu.VMEM((seq, LANES), F32),
            pltpu.VMEM((seq, LANES), F32),
        ],
        compiler_params=_params(("parallel", "arbitrary")),
        name="ssd",
    )(proj3, proj3, proj3, dt3, conv_w, conv_w, conv_w, conv_b, conv_b, conv_b,
      dtb, alog, dskip, norm_w, expand)


CM_CONV_ROWS = 64
CM_CONV_LANES = 512
CM_SHIFT_ROWS = 32


def _convmod_kernel(v_ref, g_ref, vp_ref, gp_ref, vn_ref, gn_ref, w_ref, b_ref, lg_ref, lb_ref,
                    o_ref, u_s, c_s, sh_s, wb_s):
    tt, width = v_ref.shape
    t = pl.program_id(1)
    nt = pl.num_programs(1)
    half = (CM_CONV - 1) // 2

    def glu(v, g):
        return v * jax.nn.sigmoid(g)

    prev = glu(vp_ref[...], gp_ref[...])
    u_s[0:CM_HALO, :] = jnp.where(t > 0, prev, jnp.zeros_like(prev))
    nxt = glu(vn_ref[...], gn_ref[...])
    u_s[CM_HALO + tt:2 * CM_HALO + tt, :] = jnp.where(t < nt - 1, nxt, jnp.zeros_like(nxt))

    def glu_body(r, carry):
        r0 = pl.multiple_of(r * LN_ROW_BLOCK, LN_ROW_BLOCK)
        rows = pl.ds(r0, LN_ROW_BLOCK)
        u_s[pl.ds(CM_HALO + r0, LN_ROW_BLOCK), :] = glu(v_ref[rows, :], g_ref[rows, :])
        return carry

    lax.fori_loop(0, tt // LN_ROW_BLOCK, glu_body, 0)
    u_s[2 * CM_HALO + tt:2 * CM_HALO + tt + SUBLANES, :] = jnp.zeros((SUBLANES, width), F32)

    for cb in range(width // CM_CONV_LANES):
        lanes = slice(cb * CM_CONV_LANES, (cb + 1) * CM_CONV_LANES)

        def shift_body(r, carry, lanes=lanes):
            r0 = pl.multiple_of(r * CM_SHIFT_ROWS, CM_SHIFT_ROWS)
            win = u_s[pl.ds(r0, CM_SHIFT_ROWS + SUBLANES), lanes]
            for s in range(SUBLANES):
                sh_s[s, pl.ds(r0, CM_SHIFT_ROWS), :] = win[s:s + CM_SHIFT_ROWS, :]
            return carry

        lax.fori_loop(0, (tt + 2 * CM_HALO) // CM_SHIFT_ROWS, shift_body, 0)

        for k in range(CM_CONV):
            wb_s[k] = jnp.broadcast_to(w_ref[k:k + 1, lanes], (SUBLANES, CM_CONV_LANES))

        def conv_body(r, carry, lanes=lanes):
            r0 = pl.multiple_of(r * CM_CONV_ROWS, CM_CONV_ROWS)
            groups = CM_CONV_ROWS // SUBLANES
            acc = jnp.broadcast_to(b_ref[:, lanes], (groups, SUBLANES, CM_CONV_LANES))
            for k in range(CM_CONV):
                off = CM_HALO - half + k
                win = sh_s[off % SUBLANES, pl.ds(r0 + (off // SUBLANES) * SUBLANES, CM_CONV_ROWS), :]
                acc = acc + wb_s[k][None] * win.reshape(groups, SUBLANES, CM_CONV_LANES)
            c_s[pl.ds(r0, CM_CONV_ROWS), lanes] = acc.reshape(CM_CONV_ROWS, CM_CONV_LANES)
            return carry

        lax.fori_loop(0, tt // CM_CONV_ROWS, conv_body, 0)

    lg = lg_ref[...]
    lb = lb_ref[...]

    def ln_body(r, carry):
        rows = pl.ds(pl.multiple_of(r * LN_ROW_BLOCK, LN_ROW_BLOCK), LN_ROW_BLOCK)
        y = c_s[rows, :]
        mu = jnp.mean(y, axis=-1, keepdims=True)
        d = y - mu
        var = jnp.mean(d * d, axis=-1, keepdims=True)
        o_ref[rows, :] = _silu(d * lax.rsqrt(var + LN_EPS) * lg + lb).astype(BF16)
        return carry

    lax.fori_loop(0, tt // LN_ROW_BLOCK, ln_body, 0, unroll=4)


def _convmod(proj3, conv_w, conv_b, ln_g, ln_b, layer, col, rows_per_step):
    nb, seq, _ = proj3.shape
    width = conv_w.shape[-1]
    tt = min(rows_per_step, seq)
    halo_per_tile = tt // CM_HALO
    last_halo = seq // CM_HALO - 1
    v_blk, g_blk = col["cm_val"] // width, col["cm_gate"] // width

    def cur(blk):
        return pl.BlockSpec((None, tt, width), lambda b, t: (b, t, blk))

    def before(blk):
        return pl.BlockSpec((None, CM_HALO, width),
                            lambda b, t: (b, jnp.maximum(t * halo_per_tile - 1, 0), blk))

    def after(blk):
        return pl.BlockSpec((None, CM_HALO, width),
                            lambda b, t: (b, jnp.minimum((t + 1) * halo_per_tile, last_halo), blk))

    def per_layer(rows):
        return pl.BlockSpec((None, rows, width), lambda b, t: (layer, 0, 0))

    return pl.pallas_call(
        _convmod_kernel,
        grid=(nb, seq // tt),
        in_specs=[cur(v_blk), cur(g_blk), before(v_blk), before(g_blk), after(v_blk), after(g_blk),
                  per_layer(CM_CONV), per_layer(1), per_layer(1), per_layer(1)],
        out_specs=pl.BlockSpec((None, tt, width), lambda b, t: (b, t, 0)),
        out_shape=jax.ShapeDtypeStruct((nb, seq, width), BF16),
        scratch_shapes=[
            pltpu.VMEM((tt + 2 * CM_HALO + SUBLANES, width), F32),
            pltpu.VMEM((tt, width), F32),
            pltpu.VMEM((SUBLANES, tt + 2 * CM_HALO, CM_CONV_LANES), F32),
            pltpu.VMEM((CM_CONV, SUBLANES, CM_CONV_LANES), F32),
        ],
        compiler_params=_params(("parallel", "arbitrary")),
        name="convmod",
    )(proj3, proj3, proj3, proj3, proj3, proj3, conv_w, conv_b, ln_g, ln_b)


def _outproj_kernel(xc_ref, a_ref, u_ref, w_ref, g_ref, b_ref, o32_ref, ob_ref, acc_s, *,
                    alpha, ksplit, nk):
    k = pl.program_id(1)

    @pl.when(k == 0)
    def _():
        acc_s[...] = jnp.zeros_like(acc_s)

    @pl.when(k < nk)
    def _():
        lhs = jnp.where(k < ksplit, a_ref[...], u_ref[...])
        acc_s[...] += jnp.dot(lhs, w_ref[...], preferred_element_type=F32)

    _add_residual_chunk(k, xc_ref, acc_s, alpha)

    @pl.when(k >= nk)
    def _():
        _layernorm_slab(k - nk, 1.0, acc_s, g_ref, b_ref, o32_ref, ob_ref)


def _outproj_ln(x32, a, u, w, g, b, layer, alpha):
    t, d = x32.shape
    ka = a.shape[-1]
    tm = min(OUT_ROWS, t)
    tk = OUT_K
    ksplit = ka // tk
    nk = w.shape[1] // tk
    assert nk >= RES_CHUNKS and d % RES_CHUNKS == 0 and tm % (LN_SLABS * LN_ROW_BLOCK) == 0
    cw = d // RES_CHUNKS
    slab = tm // LN_SLABS
    return pl.pallas_call(
        functools.partial(_outproj_kernel, alpha=alpha, ksplit=ksplit, nk=nk),
        grid=(t // tm, nk + LN_SLABS),
        in_specs=[
            pl.BlockSpec((tm, cw), lambda i, k: (i, jnp.minimum(k, RES_CHUNKS - 1))),
            pl.BlockSpec((tm, tk), lambda i, k: (i, jnp.minimum(k, ksplit - 1))),
            pl.BlockSpec((tm, tk), lambda i, k: (i, jnp.clip(k - ksplit, 0, ksplit - 1))),
            pl.BlockSpec((None, tk, d), lambda i, k: (layer, jnp.minimum(k, nk - 1), 0)),
            pl.BlockSpec((None, 1, d), lambda i, k: (layer, 0, 0)),
            pl.BlockSpec((None, 1, d), lambda i, k: (layer, 0, 0)),
        ],
        out_specs=[
            pl.BlockSpec((slab, d), lambda i, k: _slab_index(i, k, nk)),
            pl.BlockSpec((slab, d), lambda i, k: _slab_index(i, k, nk)),
        ],
        out_shape=[jax.ShapeDtypeStruct((t, d), F32), jax.ShapeDtypeStruct((t, d), BF16)],
        scratch_shapes=[pltpu.VMEM((tm, d), F32)],
        compiler_params=_params(("parallel", "arbitrary")),
        name="outproj_ln",
    )(x32, a, u, w, g, b)


def _column_layout(d):
    return {"z": 0, "cm_val": d, "cm_gate": 2 * d, "xs": 3 * d, "bc": 4 * d}


def _group_major(fwd, bwd):
    nl = fwd.shape[0]
    both = jnp.stack([fwd.reshape(nl, GROUPS, HEADS_PER_GROUP),
                      bwd.reshape(nl, GROUPS, HEADS_PER_GROUP)], axis=2)
    return both.reshape(nl, 1, 2 * GROUPS * HEADS_PER_GROUP).astype(F32)


def kernel(x_prompt, x_sample, ffn1_w_gate, ffn1_w_up, ffn1_w_down, ffn1_ln_g, ffn1_ln_b, w_in, ssd_conv_w, ssd_conv_b, dt_bias_fwd, dt_bias_bwd, a_log_fwd, a_log_bwd, d_skip, ssd_norm_w, cm_conv_w, cm_conv_b, cm_ln_g, cm_ln_b, w_out, mix_ln_g, mix_ln_b, ffn2_w_gate, ffn2_w_up, ffn2_w_down, ffn2_ln_g, ffn2_ln_b):
    depth, d, _ = w_in.shape
    alpha = float((2 * depth) ** 0.25)
    nbp, seq, _ = x_prompt.shape
    nbs = x_sample.shape[0]
    assert x_sample.shape[1] == seq and seq % CHUNK == 0
    nb = nbp + nbs
    heads = GROUPS * HEADS_PER_GROUP
    width = GROUPS * GROUP_WIDTH
    bcw = GROUPS * STATE
    xbc = width + 2 * bcw
    assert d == width and w_in.shape[-1] == width + xbc + 2 * heads + 2 * d

    cast = lambda w: w.astype(BF16)
    row = lambda p: p.reshape(depth, 1, -1).astype(F32)
    f1g, f1u, f1d = cast(ffn1_w_gate), cast(ffn1_w_up), cast(ffn1_w_down)
    f2g, f2u, f2d = cast(ffn2_w_gate), cast(ffn2_w_up), cast(ffn2_w_down)
    wo = cast(w_out)
    o_xbc, o_dt, o_cm = width, width + xbc, width + xbc + 2 * heads
    o_b, o_c = o_xbc + width, o_xbc + width + bcw
    w_in_b = cast(w_in)
    w_b = w_in_b[:, :, o_b:o_c].reshape(depth, d, GROUPS, STATE)
    w_c = w_in_b[:, :, o_c:o_dt].reshape(depth, d, GROUPS, STATE)
    w_bc = jnp.concatenate([w_b, w_c], axis=-1).reshape(depth, d, 2 * bcw)
    w_main = jnp.concatenate(
        [w_in_b[:, :, :o_xbc], w_in_b[:, :, o_cm:], w_in_b[:, :, o_xbc:o_b], w_bc], axis=-1)
    perm = np.arange(2 * heads).reshape(2, GROUPS, HEADS_PER_GROUP).transpose(1, 0, 2).reshape(-1)
    w_dt = w_in_b[:, :, o_dt:o_cm][:, :, perm]
    dtb = _group_major(dt_bias_fwd, dt_bias_bwd)
    alog = _group_major(a_log_fwd, a_log_bwd)
    dskip = jnp.repeat(d_skip.astype(F32), HEAD_DIM, axis=-1).reshape(depth, 1, width)
    expand = jnp.asarray(_expand_matrix(), BF16)
    col = _column_layout(d)

    def trunk(x):
        nb = x.shape[0]
        x32 = x.reshape(nb * seq, d)
        xb = x32.astype(BF16)
        for l in range(depth):
            x32, xb = _ffn_ln(x32, xb, f1g, f1u, f1d, row(ffn1_ln_g), row(ffn1_ln_b), l, alpha)
            proj, dt_raw = _inproj(xb, w_main, w_dt, l)
            proj3 = proj.reshape(nb, seq, -1)
            ssd_out = _ssd(proj3, dt_raw.reshape(nb, seq, -1), ssd_conv_w, row(ssd_conv_b), dtb, alog,
                           dskip, row(ssd_norm_w), expand, l, col)
            u = _convmod(proj3, cm_conv_w, row(cm_conv_b), row(cm_ln_g), row(cm_ln_b), l, col, CM_ROWS)
            x32, xb = _outproj_ln(x32, ssd_out.reshape(nb * seq, width), u.reshape(nb * seq, d), wo,
                                  row(mix_ln_g), row(mix_ln_b), l, alpha)
            x32, xb = _ffn_ln(x32, xb, f2g, f2u, f2d, row(ffn2_ln_g), row(ffn2_ln_b), l, alpha)
        return x32.reshape(nb, seq, d)

    return (trunk(x_prompt), trunk(x_sample))
```

```python
import functools

import jax
import jax.numpy as jnp
import numpy as np
from jax import lax
from jax.experimental import pallas as pl
from jax.experimental.pallas import tpu as pltpu

F32 = jnp.float32
BF16 = jnp.bfloat16

HEAD_DIM = 64
GROUPS = 8
HEADS_PER_GROUP = 8
GROUP_WIDTH = HEAD_DIM * HEADS_PER_GROUP
STATE = 128
SSD_CONV = 5
CHUNK = 128
CM_CONV = 31
LN_EPS = 1e-5
RMS_EPS = 1e-5

LANES = 128
SUBLANES = 8
VMEM_LIMIT_BYTES = 60 * 1024 * 1024

FFN_ROWS = 1024
FFN_COLS = 256
PROJ_ROWS = 1024
PROJ_COLS = 1024
OUT_ROWS = 1024
OUT_K = 1024
CM_ROWS = 256
CM_HALO = 16
LN_ROW_BLOCK = 16
RES_CHUNKS = 8
RES_ROW_BLOCK = 64
LN_SLABS = 8


def _params(sem):
    return pltpu.CompilerParams(dimension_semantics=sem, vmem_limit_bytes=VMEM_LIMIT_BYTES)


def _silu(v):
    return v * jax.nn.sigmoid(v)


def _add_residual_chunk(step, xc_ref, acc_ref, coeff):
    cw = xc_ref.shape[1]
    for c in range(RES_CHUNKS):
        @pl.when(step == c)
        def _(c=c):
            def body(r, carry):
                rows = pl.ds(pl.multiple_of(r * RES_ROW_BLOCK, RES_ROW_BLOCK), RES_ROW_BLOCK)
                acc_ref[rows, c * cw:(c + 1) * cw] += coeff * xc_ref[rows, :]
                return carry

            lax.fori_loop(0, acc_ref.shape[0] // RES_ROW_BLOCK, body, 0)


def _layernorm_slab(slab, scale, acc_ref, g_ref, b_ref, o32_ref, ob_ref):
    g = g_ref[...]
    b = b_ref[...]
    n = o32_ref.shape[0]
    base = slab * n

    def body(r, carry):
        r0 = pl.multiple_of(r * LN_ROW_BLOCK, LN_ROW_BLOCK)
        y = acc_ref[pl.ds(pl.multiple_of(base + r0, LN_ROW_BLOCK), LN_ROW_BLOCK), :]
        if scale != 1.0:
            y = scale * y
        mu = jnp.mean(y, axis=-1, keepdims=True)
        d = y - mu
        var = jnp.mean(d * d, axis=-1, keepdims=True)
        out = d * lax.rsqrt(var + LN_EPS) * g + b
        o32_ref[pl.ds(r0, LN_ROW_BLOCK), :] = out
        ob_ref[pl.ds(r0, LN_ROW_BLOCK), :] = out.astype(BF16)
        return carry

    lax.fori_loop(0, n // LN_ROW_BLOCK, body, 0, unroll=8)


def _ffn_kernel(xc_ref, xb_ref, wg_ref, wu_ref, wd_ref, g_ref, b_ref, o32_ref, ob_ref, acc_s, *,
                alpha, nf):
    j = pl.program_id(1)

    @pl.when(j == 0)
    def _():
        acc_s[...] = jnp.zeros_like(acc_s)

    @pl.when(j < nf)
    def _():
        xb = xb_ref[...]
        gate = jnp.dot(xb, wg_ref[...], preferred_element_type=F32)
        up = jnp.dot(xb, wu_ref[...], preferred_element_type=F32)
        h = (_silu(gate) * up).astype(BF16)
        acc_s[...] += jnp.dot(h, wd_ref[...], preferred_element_type=F32)

    _add_residual_chunk(j, xc_ref, acc_s, 2.0 * alpha)

    @pl.when(j >= nf)
    def _():
        _layernorm_slab(j - nf, 0.5, acc_s, g_ref, b_ref, o32_ref, ob_ref)


def _slab_index(i, step, first):
    return (i * LN_SLABS + jnp.clip(step - first, 0, LN_SLABS - 1), 0)


def _ffn_ln(x32, xb, wg, wu, wd, g, b, layer, alpha):
    t, d = x32.shape
    f = wg.shape[-1]
    tm = min(FFN_ROWS, t)
    tf = FFN_COLS
    nf = f // tf
    assert nf >= RES_CHUNKS and d % RES_CHUNKS == 0 and tm % (LN_SLABS * LN_ROW_BLOCK) == 0
    cw = d // RES_CHUNKS
    slab = tm // LN_SLABS
    last = nf - 1
    return pl.pallas_call(
        functools.partial(_ffn_kernel, alpha=alpha, nf=nf),
        grid=(t // tm, nf + LN_SLABS),
        in_specs=[
            pl.BlockSpec((tm, cw), lambda i, j: (i, jnp.minimum(j, RES_CHUNKS - 1))),
            pl.BlockSpec((tm, d), lambda i, j: (i, 0)),
            pl.BlockSpec((None, d, tf), lambda i, j: (layer, 0, jnp.minimum(j, last))),
            pl.BlockSpec((None, d, tf), lambda i, j: (layer, 0, jnp.minimum(j, last))),
            pl.BlockSpec((None, tf, d), lambda i, j: (layer, jnp.minimum(j, last), 0)),
            pl.BlockSpec((None, 1, d), lambda i, j: (layer, 0, 0)),
            pl.BlockSpec((None, 1, d), lambda i, j: (layer, 0, 0)),
        ],
        out_specs=[
            pl.BlockSpec((slab, d), lambda i, j: _slab_index(i, j, nf)),
            pl.BlockSpec((slab, d), lambda i, j: _slab_index(i, j, nf)),
        ],
        out_shape=[jax.ShapeDtypeStruct((t, d), F32), jax.ShapeDtypeStruct((t, d), BF16)],
        scratch_shapes=[pltpu.VMEM((tm, d), F32)],
        compiler_params=_params(("parallel", "arbitrary")),
        name="ffn_ln",
    )(x32, xb, wg, wu, wd, g, b)


def _inproj_kernel(xb_ref, w_ref, wdt_ref, o_ref, odt_ref):
    xb = xb_ref[...]
    o_ref[...] = jnp.dot(xb, w_ref[...], preferred_element_type=F32)

    @pl.when(pl.program_id(1) == 0)
    def _():
        odt_ref[...] = jnp.dot(xb, wdt_ref[...], preferred_element_type=F32)


def _inproj(xb, w_main, w_dt, layer):
    t, d = xb.shape
    n = w_main.shape[-1]
    ndt = w_dt.shape[-1]
    tm = min(PROJ_ROWS, t)
    tn = PROJ_COLS
    return pl.pallas_call(
        _inproj_kernel,
        grid=(t // tm, n // tn),
        in_specs=[
            pl.BlockSpec((tm, d), lambda i, j: (i, 0)),
            pl.BlockSpec((None, d, tn), lambda i, j: (layer, 0, j)),
            pl.BlockSpec((None, d, ndt), lambda i, j: (layer, 0, 0)),
        ],
        out_specs=[
            pl.BlockSpec((tm, tn), lambda i, j: (i, j)),
            pl.BlockSpec((tm, ndt), lambda i, j: (i, 0)),
        ],
        out_shape=[jax.ShapeDtypeStruct((t, n), F32), jax.ShapeDtypeStruct((t, ndt), F32)],
        compiler_params=_params(("parallel", "arbitrary")),
        name="inproj",
    )(xb, w_main, w_dt)


SSD_PAD = 8
SSD_CONV_ROWS = 64
SSD_BC_CONV_ROWS = 128
SSD_OUT_ROWS = 64
EXPAND_FACTORS = 2
PIECE_LANES = 64


def _expand_matrix():
    e = np.zeros((2, 2 * LANES, EXPAND_FACTORS * GROUP_WIDTH), np.float32)
    for d in range(2):
        for q in range(EXPAND_FACTORS):
            for h in range(HEADS_PER_GROUP):
                lane = 2 * HEADS_PER_GROUP * q + HEADS_PER_GROUP * d + h
                cols = slice(GROUP_WIDTH * q + HEAD_DIM * h, GROUP_WIDTH * q + HEAD_DIM * (h + 1))
                for piece in range(3):
                    e[d, PIECE_LANES * piece + lane, cols] = 1.0
    return e


def _ssd_kernel(xs_ref, bcm_ref, z_ref, dt_ref, cwx_ref, cwb_ref, cwc_ref,
                cbx_ref, cbb_ref, cbc_ref, dtb_ref, alog_ref, dskip_ref, nw_ref, e_ref,
                o_ref,
                pad_s, xs_s, bf_s, bc_s, bt_s, cc_s, acs_s, rowf_s, lhs_s, y_s, h_s,
                acsall_s, toend_s, fstart_s):
    seq = xs_ref.shape[0]
    nchunks = seq // CHUNK
    grp = pl.program_id(1)
    half = (SSD_CONV - 1) // 2

    def conv_silu(src_ref, w, b, width, nrows, store):
        pad_s[0:SSD_PAD, 0:width] = jnp.zeros((SSD_PAD, width), F32)
        pad_s[SSD_PAD + seq:2 * SSD_PAD + seq, 0:width] = jnp.zeros((SSD_PAD, width), F32)

        def copy_body(r, carry):
            r0 = pl.multiple_of(r * CHUNK, CHUNK)
            pad_s[pl.ds(SSD_PAD + r0, CHUNK), 0:width] = src_ref[pl.ds(r0, CHUNK), :]
            return carry

        lax.fori_loop(0, seq // CHUNK, copy_body, 0)

        def conv_body(r, carry):
            r0 = pl.multiple_of(r * nrows, nrows)
            win = pad_s[pl.ds(r0, nrows + 2 * SSD_PAD), 0:width]
            acc = jnp.broadcast_to(b, (nrows, width))
            for k in range(SSD_CONV):
                off = SSD_PAD - half + k
                acc = acc + w[k:k + 1, :] * win[off:off + nrows, :]
            store(r0, _silu(acc))
            return carry

        lax.fori_loop(0, seq // nrows, conv_body, 0)

    def store_x(r0, v):
        xs_s[pl.ds(r0, SSD_CONV_ROWS), :] = v

    def store_bc(r0, v):
        bf_s[pl.ds(r0, SSD_BC_CONV_ROWS), :] = v[:, 0:STATE]
        cc_s[pl.ds(r0, SSD_BC_CONV_ROWS), :] = v[:, STATE:2 * STATE].astype(BF16)

    conv_silu(xs_ref, cwx_ref[...], cbx_ref[...], GROUP_WIDTH, SSD_CONV_ROWS, store_x)
    conv_silu(bcm_ref, jnp.concatenate([cwb_ref[...], cwc_ref[...]], axis=1),
              jnp.concatenate([cbb_ref[...], cbc_ref[...]], axis=1), 2 * STATE, SSD_BC_CONV_ROWS, store_bc)

    row_id = lax.broadcasted_iota(jnp.int32, (CHUNK, LANES), 0)
    lane_id = lax.broadcasted_iota(jnp.int32, (CHUNK, LANES), 1)
    nh2 = 2 * HEADS_PER_GROUP

    @pl.when(grp == 0)
    def _():
        neg_a = -jnp.exp(alog_ref[...])
        is_fwd = lax.rem(lane_id, nh2) < HEADS_PER_GROUP

        def all_heads_body(c, carry):
            rows = pl.ds(pl.multiple_of(c * CHUNK, CHUNK), CHUNK)
            raw = dt_ref[rows, :] + dtb_ref[...]
            dt = jnp.maximum(raw, 0.0) + jnp.log1p(jnp.exp(-jnp.abs(raw)))
            da = dt * neg_a
            fwd = da
            bwd = da
            step = 1
            while step < CHUNK:
                fwd = fwd + jnp.where(row_id >= step, pltpu.roll(fwd, step, 0), 0.0)
                bwd = bwd + jnp.where(row_id < CHUNK - step, pltpu.roll(bwd, CHUNK - step, 0), 0.0)
                step *= 2
            acs = jnp.where(is_fwd, fwd, bwd)
            last = jnp.where(is_fwd, acs[CHUNK - 1:CHUNK, :], acs[0:1, :])
            acsall_s[rows, :] = acs
            rowf_s[c] = (acs - jnp.log(dt)).T
            toend_s[rows, :] = dt * jnp.exp(last - acs)
            fstart_s[rows, :] = jnp.exp(acs)
            return carry

        lax.fori_loop(0, nchunks, all_heads_body, 0, unroll=2)

    shift0 = lax.rem(LANES - nh2 * grp, LANES)
    shift1 = lax.rem(LANES + nh2 - nh2 * grp, LANES)

    def prep_body(c, carry):
        rows = pl.ds(pl.multiple_of(c * CHUNK, CHUNK), CHUNK)
        acs = pltpu.roll(acsall_s[rows, :], shift0, 1)
        acs_s[rows, :] = acs
        q = jnp.where(lane_id < nh2, pltpu.roll(toend_s[rows, :], shift0, 1),
                      jnp.where(lane_id < 2 * nh2, pltpu.roll(fstart_s[rows, :], shift1, 1), 0.0))
        hi = q.astype(BF16).astype(F32)
        rest = q - hi
        mid = rest.astype(BF16).astype(F32)
        lo = rest - mid
        lhs_s[rows, 0:LANES] = jnp.where(lane_id < PIECE_LANES, hi, pltpu.roll(mid, PIECE_LANES, 1)).astype(BF16)
        lhs_s[rows, LANES:2 * LANES] = lo.astype(BF16)
        bchunk = bf_s[rows, :]
        bc_s[rows, :] = bchunk.astype(BF16)
        bt_s[c] = bchunk.T.astype(BF16)
        return carry

    lax.fori_loop(0, nchunks, prep_body, 0, unroll=4)

    def scan_direction(backward):
        k0 = HEADS_PER_GROUP if backward else 0
        keep = (row_id <= lane_id) if backward else (row_id >= lane_id)
        edge = 0 if backward else CHUNK - 1
        h_s[...] = jnp.zeros_like(h_s)

        def chunk_body(i, carry):
            c = (nchunks - 1 - i) if backward else i
            rows = pl.ds(pl.multiple_of(c * CHUNK, CHUNK), CHUNK)
            cc = cc_s[rows, :]
            bc = bc_s[rows, :]
            scores = lax.dot_general(cc, bc, (((1,), (1,)), ((), ())), preferred_element_type=F32)
            xs_c = xs_s[rows, :]
            acs = acs_s[rows, :]
            rowf = rowf_s[c, pl.ds(pl.multiple_of(nh2 * grp, nh2), nh2), :]
            wide = jnp.dot(lhs_s[rows, :], e_ref[1 if backward else 0], preferred_element_type=F32)
            to_end_w = wide[:, 0:GROUP_WIDTH]
            from_start_w = wide[:, GROUP_WIDTH:2 * GROUP_WIDTH]
            w_in = (xs_c * to_end_w).astype(BF16)
            y = jnp.dot(cc, h_s[...].astype(BF16), preferred_element_type=F32) * from_start_w
            parts = []
            for j in range(HEADS_PER_GROUP // 2):
                ms = []
                for h in (2 * j, 2 * j + 1):
                    seg = acs[:, k0 + h:k0 + h + 1] - rowf[k0 + h:k0 + h + 1, :]
                    decay = jnp.exp(jnp.where(keep, seg, -jnp.inf))
                    ms.append((scores * decay).astype(BF16))
                lhs = jnp.concatenate(ms, axis=1)
                pair = xs_c[:, j * LANES:(j + 1) * LANES]
                rhs = jnp.concatenate([jnp.where(lane_id < HEAD_DIM, pair, 0.0).astype(BF16),
                                       jnp.where(lane_id >= HEAD_DIM, pair, 0.0).astype(BF16)], axis=0)
                parts.append(jnp.dot(lhs, rhs, preferred_element_type=F32))
            y = y + jnp.concatenate(parts, axis=1)
            if backward:
                y_s[rows, :] += y
            else:
                y_s[rows, :] = y
            h_s[...] = h_s[...] * from_start_w[edge:edge + 1, :] + jnp.dot(
                bt_s[c], w_in, preferred_element_type=F32)
            return carry

        lax.fori_loop(0, nchunks, chunk_body, 0, unroll=4)

    scan_direction(False)
    scan_direction(True)

    def out_body(r, carry):
        rows = pl.ds(pl.multiple_of(r * SSD_OUT_ROWS, SSD_OUT_ROWS), SSD_OUT_ROWS)
        y = y_s[rows, :] + dskip_ref[...] * xs_s[rows, :]
        gy = y * _silu(z_ref[rows, :])
        ms = jnp.mean(gy * gy, axis=-1, keepdims=True)
        o_ref[rows, :] = (gy * lax.rsqrt(ms + RMS_EPS) * nw_ref[...]).astype(BF16)
        return carry

    lax.fori_loop(0, seq // SSD_OUT_ROWS, out_body, 0, unroll=4)


def _ssd(proj3, dt3, conv_w, conv_b, dtb, alog, dskip, norm_w, expand, layer, col):
    nb, seq, _ = proj3.shape
    nchunks = seq // CHUNK
    gw, st = GROUP_WIDTH, STATE
    width = GROUPS * gw
    xs_blk, bc_blk, z_blk = col["xs"] // gw, col["bc"] // (2 * st), col["z"] // gw
    cb_blk, cc_blk = width // st, (width + GROUPS * st) // st
    return pl.pallas_call(
        _ssd_kernel,
        grid=(nb, GROUPS),
        in_specs=[
            pl.BlockSpec((None, seq, gw), lambda b, g: (b, 0, xs_blk + g)),
            pl.BlockSpec((None, seq, 2 * st), lambda b, g: (b, 0, bc_blk + g)),
            pl.BlockSpec((None, seq, gw), lambda b, g: (b, 0, z_blk + g)),
            pl.BlockSpec((None, seq, LANES), lambda b, g: (b, 0, 0)),
            pl.BlockSpec((None, SSD_CONV, gw), lambda b, g: (layer, 0, g)),
            pl.BlockSpec((None, SSD_CONV, st), lambda b, g: (layer, 0, cb_blk + g)),
            pl.BlockSpec((None, SSD_CONV, st), lambda b, g: (layer, 0, cc_blk + g)),
            pl.BlockSpec((None, 1, gw), lambda b, g: (layer, 0, g)),
            pl.BlockSpec((None, 1, st), lambda b, g: (layer, 0, cb_blk + g)),
            pl.BlockSpec((None, 1, st), lambda b, g: (layer, 0, cc_blk + g)),
            pl.BlockSpec((None, 1, LANES), lambda b, g: (layer, 0, 0)),
            pl.BlockSpec((None, 1, LANES), lambda b, g: (layer, 0, 0)),
            pl.BlockSpec((None, 1, gw), lambda b, g: (layer, 0, g)),
            pl.BlockSpec((None, 1, gw), lambda b, g: (layer, 0, g)),
            pl.BlockSpec(expand.shape, lambda b, g: (0, 0, 0)),
        ],
        out_specs=pl.BlockSpec((None, seq, gw), lambda b, g: (b, 0, g)),
        out_shape=jax.ShapeDtypeStruct((nb, seq, width), BF16),
        scratch_shapes=[
            pltpu.VMEM((seq + 2 * SSD_PAD, gw), F32),
            pltpu.VMEM((seq, gw), F32),
            pltpu.VMEM((seq, st), F32),
            pltpu.VMEM((seq, st), BF16),
            pltpu.VMEM((nchunks, st, CHUNK), BF16),
            pltpu.VMEM((seq, st), BF16),
            pltpu.VMEM((seq, LANES), F32),
            pltpu.VMEM((nchunks, LANES, CHUNK), F32),
            pltpu.VMEM((seq, 2 * LANES), BF16),
            pltpu.VMEM((seq, gw), F32),
            pltpu.VMEM((st, gw), F32),
            pltpu.VMEM((seq, LANES), F32),
            pltpu.VMEM((seq, LANES), F32),
            pltpu.VMEM((seq, LANES), F32),
        ],
        compiler_params=_params(("parallel", "arbitrary")),
        name="ssd",
    )(proj3, proj3, proj3, dt3, conv_w, conv_w, conv_w, conv_b, conv_b, conv_b,
      dtb, alog, dskip, norm_w, expand)


CM_CONV_ROWS = 64
CM_CONV_LANES = 512
CM_SHIFT_ROWS = 32


def _convmod_kernel(v_ref, g_ref, vp_ref, gp_ref, vn_ref, gn_ref, w_ref, b_ref, lg_ref, lb_ref,
                    o_ref, u_s, c_s, sh_s, wb_s):
    tt, width = v_ref.shape
    t = pl.program_id(1)
    nt = pl.num_programs(1)
    half = (CM_CONV - 1) // 2

    def glu(v, g):
        return v * jax.nn.sigmoid(g)

    prev = glu(vp_ref[...], gp_ref[...])
    u_s[0:CM_HALO, :] = jnp.where(t > 0, prev, jnp.zeros_like(prev))
    nxt = glu(vn_ref[...], gn_ref[...])
    u_s[CM_HALO + tt:2 * CM_HALO + tt, :] = jnp.where(t < nt - 1, nxt, jnp.zeros_like(nxt))

    def glu_body(r, carry):
        r0 = pl.multiple_of(r * LN_ROW_BLOCK, LN_ROW_BLOCK)
        rows = pl.ds(r0, LN_ROW_BLOCK)
        u_s[pl.ds(CM_HALO + r0, LN_ROW_BLOCK), :] = glu(v_ref[rows, :], g_ref[rows, :])
        return carry

    lax.fori_loop(0, tt // LN_ROW_BLOCK, glu_body, 0)
    u_s[2 * CM_HALO + tt:2 * CM_HALO + tt + SUBLANES, :] = jnp.zeros((SUBLANES, width), F32)

    for cb in range(width // CM_CONV_LANES):
        lanes = slice(cb * CM_CONV_LANES, (cb + 1) * CM_CONV_LANES)

        def shift_body(r, carry, lanes=lanes):
            r0 = pl.multiple_of(r * CM_SHIFT_ROWS, CM_SHIFT_ROWS)
            win = u_s[pl.ds(r0, CM_SHIFT_ROWS + SUBLANES), lanes]
            for s in range(SUBLANES):
                sh_s[s, pl.ds(r0, CM_SHIFT_ROWS), :] = win[s:s + CM_SHIFT_ROWS, :]
            return carry

        lax.fori_loop(0, (tt + 2 * CM_HALO) // CM_SHIFT_ROWS, shift_body, 0)

        for k in range(CM_CONV):
            wb_s[k] = jnp.broadcast_to(w_ref[k:k + 1, lanes], (SUBLANES, CM_CONV_LANES))

        def conv_body(r, carry, lanes=lanes):
            r0 = pl.multiple_of(r * CM_CONV_ROWS, CM_CONV_ROWS)
            groups = CM_CONV_ROWS // SUBLANES
            acc = jnp.broadcast_to(b_ref[:, lanes], (groups, SUBLANES, CM_CONV_LANES))
            for k in range(CM_CONV):
                off = CM_HALO - half + k
                win = sh_s[off % SUBLANES, pl.ds(r0 + (off // SUBLANES) * SUBLANES, CM_CONV_ROWS), :]
                acc = acc + wb_s[k][None] * win.reshape(groups, SUBLANES, CM_CONV_LANES)
            c_s[pl.ds(r0, CM_CONV_ROWS), lanes] = acc.reshape(CM_CONV_ROWS, CM_CONV_LANES)
            return carry

        lax.fori_loop(0, tt // CM_CONV_ROWS, conv_body, 0)

    lg = lg_ref[...]
    lb = lb_ref[...]

    def ln_body(r, carry):
        rows = pl.ds(pl.multiple_of(r * LN_ROW_BLOCK, LN_ROW_BLOCK), LN_ROW_BLOCK)
        y = c_s[rows, :]
        mu = jnp.mean(y, axis=-1, keepdims=True)
        d = y - mu
        var = jnp.mean(d * d, axis=-1, keepdims=True)
        o_ref[rows, :] = _silu(d * lax.rsqrt(var + LN_EPS) * lg + lb).astype(BF16)
        return carry

    lax.fori_loop(0, tt // LN_ROW_BLOCK, ln_body, 0, unroll=8)


def _convmod(proj3, conv_w, conv_b, ln_g, ln_b, layer, col, rows_per_step):
    nb, seq, _ = proj3.shape
    width = conv_w.shape[-1]
    tt = min(rows_per_step, seq)
    halo_per_tile = tt // CM_HALO
    last_halo = seq // CM_HALO - 1
    v_blk, g_blk = col["cm_val"] // width, col["cm_gate"] // width

    def cur(blk):
        return pl.BlockSpec((None, tt, width), lambda b, t: (b, t, blk))

    def before(blk):
        return pl.BlockSpec((None, CM_HALO, width),
                            lambda b, t: (b, jnp.maximum(t * halo_per_tile - 1, 0), blk))

    def after(blk):
        return pl.BlockSpec((None, CM_HALO, width),
                            lambda b, t: (b, jnp.minimum((t + 1) * halo_per_tile, last_halo), blk))

    def per_layer(rows):
        return pl.BlockSpec((None, rows, width), lambda b, t: (layer, 0, 0))

    return pl.pallas_call(
        _convmod_kernel,
        grid=(nb, seq // tt),
        in_specs=[cur(v_blk), cur(g_blk), before(v_blk), before(g_blk), after(v_blk), after(g_blk),
                  per_layer(CM_CONV), per_layer(1), per_layer(1), per_layer(1)],
        out_specs=pl.BlockSpec((None, tt, width), lambda b, t: (b, t, 0)),
        out_shape=jax.ShapeDtypeStruct((nb, seq, width), BF16),
        scratch_shapes=[
            pltpu.VMEM((tt + 2 * CM_HALO + SUBLANES, width), F32),
            pltpu.VMEM((tt, width), F32),
            pltpu.VMEM((SUBLANES, tt + 2 * CM_HALO, CM_CONV_LANES), F32),
            pltpu.VMEM((CM_CONV, SUBLANES, CM_CONV_LANES), F32),
        ],
        compiler_params=_params(("parallel", "arbitrary")),
        name="convmod",
    )(proj3, proj3, proj3, proj3, proj3, proj3, conv_w, conv_b, ln_g, ln_b)


def _outproj_kernel(xc_ref, a_ref, u_ref, w_ref, g_ref, b_ref, o32_ref, ob_ref, acc_s, *,
                    alpha, ksplit, nk):
    k = pl.program_id(1)

    @pl.when(k == 0)
    def _():
        acc_s[...] = jnp.zeros_like(acc_s)

    @pl.when(k < nk)
    def _():
        lhs = jnp.where(k < ksplit, a_ref[...], u_ref[...])
        acc_s[...] += jnp.dot(lhs, w_ref[...], preferred_element_type=F32)

    _add_residual_chunk(k, xc_ref, acc_s, alpha)

    @pl.when(k >= nk)
    def _():
        _layernorm_slab(k - nk, 1.0, acc_s, g_ref, b_ref, o32_ref, ob_ref)


def _outproj_ln(x32, a, u, w, g, b, layer, alpha):
    t, d = x32.shape
    ka = a.shape[-1]
    tm = min(OUT_ROWS, t)
    tk = OUT_K
    ksplit = ka // tk
    nk = w.shape[1] // tk
    assert nk >= RES_CHUNKS and d % RES_CHUNKS == 0 and tm % (LN_SLABS * LN_ROW_BLOCK) == 0
    cw = d // RES_CHUNKS
    slab = tm // LN_SLABS
    return pl.pallas_call(
        functools.partial(_outproj_kernel, alpha=alpha, ksplit=ksplit, nk=nk),
        grid=(t // tm, nk + LN_SLABS),
        in_specs=[
            pl.BlockSpec((tm, cw), lambda i, k: (i, jnp.minimum(k, RES_CHUNKS - 1))),
            pl.BlockSpec((tm, tk), lambda i, k: (i, jnp.minimum(k, ksplit - 1))),
            pl.BlockSpec((tm, tk), lambda i, k: (i, jnp.clip(k - ksplit, 0, ksplit - 1))),
            pl.BlockSpec((None, tk, d), lambda i, k: (layer, jnp.minimum(k, nk - 1), 0)),
            pl.BlockSpec((None, 1, d), lambda i, k: (layer, 0, 0)),
            pl.BlockSpec((None, 1, d), lambda i, k: (layer, 0, 0)),
        ],
        out_specs=[
            pl.BlockSpec((slab, d), lambda i, k: _slab_index(i, k, nk)),
            pl.BlockSpec((slab, d), lambda i, k: _slab_index(i, k, nk)),
        ],
        out_shape=[jax.ShapeDtypeStruct((t, d), F32), jax.ShapeDtypeStruct((t, d), BF16)],
        scratch_shapes=[pltpu.VMEM((tm, d), F32)],
        compiler_params=_params(("parallel", "arbitrary")),
        name="outproj_ln",
    )(x32, a, u, w, g, b)


def _column_layout(d):
    return {"z": 0, "cm_val": d, "cm_gate": 2 * d, "xs": 3 * d, "bc": 4 * d}


def _group_major(fwd, bwd):
    nl = fwd.shape[0]
    both = jnp.stack([fwd.reshape(nl, GROUPS, HEADS_PER_GROUP),
                      bwd.reshape(nl, GROUPS, HEADS_PER_GROUP)], axis=2)
    return both.reshape(nl, 1, 2 * GROUPS * HEADS_PER_GROUP).astype(F32)


def kernel(x_prompt, x_sample, ffn1_w_gate, ffn1_w_up, ffn1_w_down, ffn1_ln_g, ffn1_ln_b, w_in, ssd_conv_w, ssd_conv_b, dt_bias_fwd, dt_bias_bwd, a_log_fwd, a_log_bwd, d_skip, ssd_norm_w, cm_conv_w, cm_conv_b, cm_ln_g, cm_ln_b, w_out, mix_ln_g, mix_ln_b, ffn2_w_gate, ffn2_w_up, ffn2_w_down, ffn2_ln_g, ffn2_ln_b):
    depth, d, _ = w_in.shape
    alpha = float((2 * depth) ** 0.25)
    nbp, seq, _ = x_prompt.shape
    nbs = x_sample.shape[0]
    assert x_sample.shape[1] == seq and seq % CHUNK == 0
    nb = nbp + nbs
    heads = GROUPS * HEADS_PER_GROUP
    width = GROUPS * GROUP_WIDTH
    bcw = GROUPS * STATE
    xbc = width + 2 * bcw
    assert d == width and w_in.shape[-1] == width + xbc + 2 * heads + 2 * d

    cast = lambda w: w.astype(BF16)
    row = lambda p: p.reshape(depth, 1, -1).astype(F32)
    f1g, f1u, f1d = cast(ffn1_w_gate), cast(ffn1_w_up), cast(ffn1_w_down)
    f2g, f2u, f2d = cast(ffn2_w_gate), cast(ffn2_w_up), cast(ffn2_w_down)
    wo = cast(w_out)
    o_xbc, o_dt, o_cm = width, width + xbc, width + xbc + 2 * heads
    o_b, o_c = o_xbc + width, o_xbc + width + bcw
    w_in_b = cast(w_in)
    w_b = w_in_b[:, :, o_b:o_c].reshape(depth, d, GROUPS, STATE)
    w_c = w_in_b[:, :, o_c:o_dt].reshape(depth, d, GROUPS, STATE)
    w_bc = jnp.concatenate([w_b, w_c], axis=-1).reshape(depth, d, 2 * bcw)
    w_main = jnp.concatenate(
        [w_in_b[:, :, :o_xbc], w_in_b[:, :, o_cm:], w_in_b[:, :, o_xbc:o_b], w_bc], axis=-1)
    perm = np.arange(2 * heads).reshape(2, GROUPS, HEADS_PER_GROUP).transpose(1, 0, 2).reshape(-1)
    w_dt = w_in_b[:, :, o_dt:o_cm][:, :, perm]
    dtb = _group_major(dt_bias_fwd, dt_bias_bwd)
    alog = _group_major(a_log_fwd, a_log_bwd)
    dskip = jnp.repeat(d_skip.astype(F32), HEAD_DIM, axis=-1).reshape(depth, 1, width)
    expand = jnp.asarray(_expand_matrix(), BF16)
    col = _column_layout(d)

    def trunk(x):
        nb = x.shape[0]
        x32 = x.reshape(nb * seq, d)
        xb = x32.astype(BF16)
        for l in range(depth):
            x32, xb = _ffn_ln(x32, xb, f1g, f1u, f1d, row(ffn1_ln_g), row(ffn1_ln_b), l, alpha)
            proj, dt_raw = _inproj(xb, w_main, w_dt, l)
            proj3 = proj.reshape(nb, seq, -1)
            ssd_out = _ssd(proj3, dt_raw.reshape(nb, seq, -1), ssd_conv_w, row(ssd_conv_b), dtb, alog,
                           dskip, row(ssd_norm_w), expand, l, col)
            u = _convmod(proj3, cm_conv_w, row(cm_conv_b), row(cm_ln_g), row(cm_ln_b), l, col, CM_ROWS)
            x32, xb = _outproj_ln(x32, ssd_out.reshape(nb * seq, width), u.reshape(nb * seq, d), wo,
                                  row(mix_ln_g), row(mix_ln_b), l, alpha)
            x32, xb = _ffn_ln(x32, xb, f2g, f2u, f2d, row(ffn2_ln_g), row(ffn2_ln_b), l, alpha)
        return x32.reshape(nb, seq, d)

    return (trunk(x_prompt), trunk(x_sample))
```

```python
import functools

import jax
import jax.numpy as jnp
import numpy as np
from jax import lax
from jax.experimental import pallas as pl
from jax.experimental.pallas import tpu as pltpu

F32 = jnp.float32
BF16 = jnp.bfloat16

HEAD_DIM = 64
GROUPS = 8
HEADS_PER_GROUP = 8
GROUP_WIDTH = HEAD_DIM * HEADS_PER_GROUP
STATE = 128
SSD_CONV = 5
CHUNK = 128
CM_CONV = 31
LN_EPS = 1e-5
RMS_EPS = 1e-5

LANES = 128
SUBLANES = 8
VMEM_LIMIT_BYTES = 60 * 1024 * 1024

FFN_ROWS = 1024
FFN_COLS = 256
PROJ_ROWS = 1024
PROJ_COLS = 1024
OUT_ROWS = 1024
OUT_K = 1024
CM_ROWS = 256
CM_HALO = 16
LN_ROW_BLOCK = 16
RES_CHUNKS = 8
RES_ROW_BLOCK = 64
LN_SLABS = 8


def _params(sem):
    return pltpu.CompilerParams(dimension_semantics=sem, vmem_limit_bytes=VMEM_LIMIT_BYTES)


def _silu(v):
    return v * jax.nn.sigmoid(v)


def _add_residual_chunk(step, xc_ref, acc_ref, coeff):
    cw = xc_ref.shape[1]
    for c in range(RES_CHUNKS):
        @pl.when(step == c)
        def _(c=c):
            def body(r, carry):
                rows = pl.ds(pl.multiple_of(r * RES_ROW_BLOCK, RES_ROW_BLOCK), RES_ROW_BLOCK)
                acc_ref[rows, c * cw:(c + 1) * cw] += coeff * xc_ref[rows, :]
                return carry

            lax.fori_loop(0, acc_ref.shape[0] // RES_ROW_BLOCK, body, 0)


def _layernorm_slab(slab, scale, acc_ref, g_ref, b_ref, o32_ref, ob_ref):
    g = g_ref[...]
    b = b_ref[...]
    n = o32_ref.shape[0]
    base = slab * n

    def body(r, carry):
        r0 = pl.multiple_of(r * LN_ROW_BLOCK, LN_ROW_BLOCK)
        y = acc_ref[pl.ds(pl.multiple_of(base + r0, LN_ROW_BLOCK), LN_ROW_BLOCK), :]
        if scale != 1.0:
            y = scale * y
        mu = jnp.mean(y, axis=-1, keepdims=True)
        d = y - mu
        var = jnp.mean(d * d, axis=-1, keepdims=True)
        out = d * lax.rsqrt(var + LN_EPS) * g + b
        o32_ref[pl.ds(r0, LN_ROW_BLOCK), :] = out
        ob_ref[pl.ds(r0, LN_ROW_BLOCK), :] = out.astype(BF16)
        return carry

    lax.fori_loop(0, n // LN_ROW_BLOCK, body, 0, unroll=8)


def _ffn_kernel(xc_ref, xb_ref, wgu_ref, wd_ref, g_ref, b_ref, o32_ref, ob_ref, acc_s, *,
                alpha, nf):
    j = pl.program_id(1)
    tf = wd_ref.shape[0]

    @pl.when(j == 0)
    def _():
        acc_s[...] = jnp.zeros_like(acc_s)

    @pl.when(j < nf)
    def _():
        gate_up = jnp.dot(xb_ref[...], wgu_ref[...], preferred_element_type=F32)
        gate = gate_up[:, 0:tf]
        up = gate_up[:, tf:2 * tf]
        h = (_silu(gate) * up).astype(BF16)
        acc_s[...] += jnp.dot(h, wd_ref[...], preferred_element_type=F32)

    _add_residual_chunk(j, xc_ref, acc_s, 2.0 * alpha)

    @pl.when(j >= nf)
    def _():
        _layernorm_slab(j - nf, 0.5, acc_s, g_ref, b_ref, o32_ref, ob_ref)


def _slab_index(i, step, first):
    return (i * LN_SLABS + jnp.clip(step - first, 0, LN_SLABS - 1), 0)


def _interleave_gate_up(wg, wu):
    nl, d, f = wg.shape
    tiles = lambda w: w.astype(BF16).reshape(nl, d, f // FFN_COLS, 1, FFN_COLS)
    return jnp.concatenate([tiles(wg), tiles(wu)], axis=3).reshape(nl, d, 2 * f)


def _ffn_ln(x32, xb, wgu, wd, g, b, layer, alpha):
    t, d = x32.shape
    f = wd.shape[1]
    tm = min(FFN_ROWS, t)
    tf = FFN_COLS
    nf = f // tf
    assert nf >= RES_CHUNKS and d % RES_CHUNKS == 0 and tm % (LN_SLABS * LN_ROW_BLOCK) == 0
    cw = d // RES_CHUNKS
    slab = tm // LN_SLABS
    last = nf - 1
    return pl.pallas_call(
        functools.partial(_ffn_kernel, alpha=alpha, nf=nf),
        grid=(t // tm, nf + LN_SLABS),
        in_specs=[
            pl.BlockSpec((tm, cw), lambda i, j: (i, jnp.minimum(j, RES_CHUNKS - 1))),
            pl.BlockSpec((tm, d), lambda i, j: (i, 0)),
            pl.BlockSpec((None, d, 2 * tf), lambda i, j: (layer, 0, jnp.minimum(j, last))),
            pl.BlockSpec((None, tf, d), lambda i, j: (layer, jnp.minimum(j, last), 0)),
            pl.BlockSpec((None, 1, d), lambda i, j: (layer, 0, 0)),
            pl.BlockSpec((None, 1, d), lambda i, j: (layer, 0, 0)),
        ],
        out_specs=[
            pl.BlockSpec((slab, d), lambda i, j: _slab_index(i, j, nf)),
            pl.BlockSpec((slab, d), lambda i, j: _slab_index(i, j, nf)),
        ],
        out_shape=[jax.ShapeDtypeStruct((t, d), F32), jax.ShapeDtypeStruct((t, d), BF16)],
        scratch_shapes=[pltpu.VMEM((tm, d), F32)],
        compiler_params=_params(("parallel", "arbitrary")),
        name="ffn_ln",
    )(x32, xb, wgu, wd, g, b)


def _inproj_kernel(xb_ref, w_ref, wdt_ref, o_ref, odt_ref):
    xb = xb_ref[...]
    o_ref[...] = jnp.dot(xb, w_ref[...], preferred_element_type=F32)

    @pl.when(pl.program_id(1) == 0)
    def _():
        odt_ref[...] = jnp.dot(xb, wdt_ref[...], preferred_element_type=F32)


def _inproj(xb, w_main, w_dt, layer):
    t, d = xb.shape
    n = w_main.shape[-1]
    ndt = w_dt.shape[-1]
    tm = min(PROJ_ROWS, t)
    tn = PROJ_COLS
    return pl.pallas_call(
        _inproj_kernel,
        grid=(t // tm, n // tn),
        in_specs=[
            pl.BlockSpec((tm, d), lambda i, j: (i, 0)),
            pl.BlockSpec((None, d, tn), lambda i, j: (layer, 0, j)),
            pl.BlockSpec((None, d, ndt), lambda i, j: (layer, 0, 0)),
        ],
        out_specs=[
            pl.BlockSpec((tm, tn), lambda i, j: (i, j)),
            pl.BlockSpec((tm, ndt), lambda i, j: (i, 0)),
        ],
        out_shape=[jax.ShapeDtypeStruct((t, n), F32), jax.ShapeDtypeStruct((t, ndt), F32)],
        compiler_params=_params(("parallel", "arbitrary")),
        name="inproj",
    )(xb, w_main, w_dt)


SSD_PAD = 8
SSD_CONV_ROWS = 64
SSD_BC_CONV_ROWS = 128
SSD_OUT_ROWS = 64
EXPAND_FACTORS = 2
PIECE_LANES = 64


def _expand_matrix():
    e = np.zeros((2, 2 * LANES, EXPAND_FACTORS * GROUP_WIDTH), np.float32)
    for d in range(2):
        for q in range(EXPAND_FACTORS):
            for h in range(HEADS_PER_GROUP):
                lane = 2 * HEADS_PER_GROUP * q + HEADS_PER_GROUP * d + h
                cols = slice(GROUP_WIDTH * q + HEAD_DIM * h, GROUP_WIDTH * q + HEAD_DIM * (h + 1))
                for piece in range(3):
                    e[d, PIECE_LANES * piece + lane, cols] = 1.0
    return e


def _ssd_kernel(xs_ref, bcm_ref, z_ref, dt_ref, cwx_ref, cwb_ref, cwc_ref,
                cbx_ref, cbb_ref, cbc_ref, dtb_ref, alog_ref, dskip_ref, nw_ref, e_ref,
                o_ref,
                pad_s, xs_s, bf_s, bc_s, bt_s, cc_s, acs_s, rowf_s, lhs_s, y_s, h_s,
                acsall_s, toend_s, fstart_s):
    seq = xs_ref.shape[0]
    nchunks = seq // CHUNK
    grp = pl.program_id(1)
    half = (SSD_CONV - 1) // 2

    def conv_silu(src_ref, w, b, width, nrows, store):
        pad_s[0:SSD_PAD, 0:width] = jnp.zeros((SSD_PAD, width), F32)
        pad_s[SSD_PAD + seq:2 * SSD_PAD + seq, 0:width] = jnp.zeros((SSD_PAD, width), F32)

        def copy_body(r, carry):
            r0 = pl.multiple_of(r * CHUNK, CHUNK)
            pad_s[pl.ds(SSD_PAD + r0, CHUNK), 0:width] = src_ref[pl.ds(r0, CHUNK), :]
            return carry

        lax.fori_loop(0, seq // CHUNK, copy_body, 0)

        def conv_body(r, carry):
            r0 = pl.multiple_of(r * nrows, nrows)
            win = pad_s[pl.ds(r0, nrows + 2 * SSD_PAD), 0:width]
            acc = jnp.broadcast_to(b, (nrows, width))
            for k in range(SSD_CONV):
                off = SSD_PAD - half + k
                acc = acc + w[k:k + 1, :] * win[off:off + nrows, :]
            store(r0, _silu(acc))
            return carry

        lax.fori_loop(0, seq // nrows, conv_body, 0)

    def store_x(r0, v):
        xs_s[pl.ds(r0, SSD_CONV_ROWS), :] = v

    def store_bc(r0, v):
        bf_s[pl.ds(r0, SSD_BC_CONV_ROWS), :] = v[:, 0:STATE]
        cc_s[pl.ds(r0, SSD_BC_CONV_ROWS), :] = v[:, STATE:2 * STATE].astype(BF16)

    conv_silu(xs_ref, cwx_ref[...], cbx_ref[...], GROUP_WIDTH, SSD_CONV_ROWS, store_x)
    conv_silu(bcm_ref, jnp.concatenate([cwb_ref[...], cwc_ref[...]], axis=1),
              jnp.concatenate([cbb_ref[...], cbc_ref[...]], axis=1), 2 * STATE, SSD_BC_CONV_ROWS, store_bc)

    row_id = lax.broadcasted_iota(jnp.int32, (CHUNK, LANES), 0)
    lane_id = lax.broadcasted_iota(jnp.int32, (CHUNK, LANES), 1)
    nh2 = 2 * HEADS_PER_GROUP

    @pl.when(grp == 0)
    def _():
        neg_a = -jnp.exp(alog_ref[...])
        is_fwd = lax.rem(lane_id, nh2) < HEADS_PER_GROUP

        def all_heads_body(c, carry):
            rows = pl.ds(pl.multiple_of(c * CHUNK, CHUNK), CHUNK)
            raw = dt_ref[rows, :] + dtb_ref[...]
            dt = jnp.maximum(raw, 0.0) + jnp.log1p(jnp.exp(-jnp.abs(raw)))
            da = dt * neg_a
            fwd = da
            bwd = da
            step = 1
            while step < CHUNK:
                fwd = fwd + jnp.where(row_id >= step, pltpu.roll(fwd, step, 0), 0.0)
                bwd = bwd + jnp.where(row_id < CHUNK - step, pltpu.roll(bwd, CHUNK - step, 0), 0.0)
                step *= 2
            acs = jnp.where(is_fwd, fwd, bwd)
            last = jnp.where(is_fwd, acs[CHUNK - 1:CHUNK, :], acs[0:1, :])
            acsall_s[rows, :] = acs
            rowf_s[c] = (acs - jnp.log(dt)).T
            toend_s[rows, :] = dt * jnp.exp(last - acs)
            fstart_s[rows, :] = jnp.exp(acs)
            return carry

        lax.fori_loop(0, nchunks, all_heads_body, 0, unroll=2)

    shift0 = lax.rem(LANES - nh2 * grp, LANES)
    shift1 = lax.rem(LANES + nh2 - nh2 * grp, LANES)

    def prep_body(c, carry):
        rows = pl.ds(pl.multiple_of(c * CHUNK, CHUNK), CHUNK)
        acs = pltpu.roll(acsall_s[rows, :], shift0, 1)
        acs_s[rows, :] = acs
        q = jnp.where(lane_id < nh2, pltpu.roll(toend_s[rows, :], shift0, 1),
                      jnp.where(lane_id < 2 * nh2, pltpu.roll(fstart_s[rows, :], shift1, 1), 0.0))
        hi = q.astype(BF16).astype(F32)
        rest = q - hi
        mid = rest.astype(BF16).astype(F32)
        lo = rest - mid
        lhs_s[rows, 0:LANES] = jnp.where(lane_id < PIECE_LANES, hi, pltpu.roll(mid, PIECE_LANES, 1)).astype(BF16)
        lhs_s[rows, LANES:2 * LANES] = lo.astype(BF16)
        bchunk = bf_s[rows, :]
        bc_s[rows, :] = bchunk.astype(BF16)
        bt_s[c] = bchunk.T.astype(BF16)
        return carry

    lax.fori_loop(0, nchunks, prep_body, 0, unroll=4)

    def scan_direction(backward):
        k0 = HEADS_PER_GROUP if backward else 0
        keep = (row_id <= lane_id) if backward else (row_id >= lane_id)
        edge = 0 if backward else CHUNK - 1
        h_s[...] = jnp.zeros_like(h_s)

        def chunk_body(i, carry):
            c = (nchunks - 1 - i) if backward else i
            rows = pl.ds(pl.multiple_of(c * CHUNK, CHUNK), CHUNK)
            cc = cc_s[rows, :]
            bc = bc_s[rows, :]
            scores = lax.dot_general(cc, bc, (((1,), (1,)), ((), ())), preferred_element_type=F32)
            xs_c = xs_s[rows, :]
            acs = acs_s[rows, :]
            rowf = rowf_s[c, pl.ds(pl.multiple_of(nh2 * grp, nh2), nh2), :]
            wide = jnp.dot(lhs_s[rows, :], e_ref[1 if backward else 0], preferred_element_type=F32)
            to_end_w = wide[:, 0:GROUP_WIDTH]
            from_start_w = wide[:, GROUP_WIDTH:2 * GROUP_WIDTH]
            w_in = (xs_c * to_end_w).astype(BF16)
            y = jnp.dot(cc, h_s[...].astype(BF16), preferred_element_type=F32) * from_start_w
            parts = []
            for j in range(HEADS_PER_GROUP // 2):
                ms = []
                for h in (2 * j, 2 * j + 1):
                    seg = acs[:, k0 + h:k0 + h + 1] - rowf[k0 + h:k0 + h + 1, :]
                    decay = jnp.exp(jnp.where(keep, seg, -jnp.inf))
                    ms.append((scores * decay).astype(BF16))
                lhs = jnp.concatenate(ms, axis=1)
                pair = xs_c[:, j * LANES:(j + 1) * LANES]
                rhs = jnp.concatenate([jnp.where(lane_id < HEAD_DIM, pair, 0.0).astype(BF16),
                                       jnp.where(lane_id >= HEAD_DIM, pair, 0.0).astype(BF16)], axis=0)
                parts.append(jnp.dot(lhs, rhs, preferred_element_type=F32))
            y = y + jnp.concatenate(parts, axis=1)
            if backward:
                y_s[rows, :] += y
            else:
                y_s[rows, :] = y
            h_s[...] = h_s[...] * from_start_w[edge:edge + 1, :] + jnp.dot(
                bt_s[c], w_in, preferred_element_type=F32)
            return carry

        lax.fori_loop(0, nchunks, chunk_body, 0, unroll=4)

    scan_direction(False)
    scan_direction(True)

    def out_body(r, carry):
        rows = pl.ds(pl.multiple_of(r * SSD_OUT_ROWS, SSD_OUT_ROWS), SSD_OUT_ROWS)
        y = y_s[rows, :] + dskip_ref[...] * xs_s[rows, :]
        gy = y * _silu(z_ref[rows, :])
        ms = jnp.mean(gy * gy, axis=-1, keepdims=True)
        o_ref[rows, :] = (gy * lax.rsqrt(ms + RMS_EPS) * nw_ref[...]).astype(BF16)
        return carry

    lax.fori_loop(0, seq // SSD_OUT_ROWS, out_body, 0, unroll=4)


def _ssd(proj3, dt3, conv_w, conv_b, dtb, alog, dskip, norm_w, expand, layer, col):
    nb, seq, _ = proj3.shape
    nchunks = seq // CHUNK
    gw, st = GROUP_WIDTH, STATE
    width = GROUPS * gw
    xs_blk, bc_blk, z_blk = col["xs"] // gw, col["bc"] // (2 * st), col["z"] // gw
    cb_blk, cc_blk = width // st, (width + GROUPS * st) // st
    return pl.pallas_call(
        _ssd_kernel,
        grid=(nb, GROUPS),
        in_specs=[
            pl.BlockSpec((None, seq, gw), lambda b, g: (b, 0, xs_blk + g)),
            pl.BlockSpec((None, seq, 2 * st), lambda b, g: (b, 0, bc_blk + g)),
            pl.BlockSpec((None, seq, gw), lambda b, g: (b, 0, z_blk + g)),
            pl.BlockSpec((None, seq, LANES), lambda b, g: (b, 0, 0)),
            pl.BlockSpec((None, SSD_CONV, gw), lambda b, g: (layer, 0, g)),
            pl.BlockSpec((None, SSD_CONV, st), lambda b, g: (layer, 0, cb_blk + g)),
            pl.BlockSpec((None, SSD_CONV, st), lambda b, g: (layer, 0, cc_blk + g)),
            pl.BlockSpec((None, 1, gw), lambda b, g: (layer, 0, g)),
            pl.BlockSpec((None, 1, st), lambda b, g: (layer, 0, cb_blk + g)),
            pl.BlockSpec((None, 1, st), lambda b, g: (layer, 0, cc_blk + g)),
            pl.BlockSpec((None, 1, LANES), lambda b, g: (layer, 0, 0)),
            pl.BlockSpec((None, 1, LANES), lambda b, g: (layer, 0, 0)),
            pl.BlockSpec((None, 1, gw), lambda b, g: (layer, 0, g)),
            pl.BlockSpec((None, 1, gw), lambda b, g: (layer, 0, g)),
            pl.BlockSpec(expand.shape, lambda b, g: (0, 0, 0)),
        ],
        out_specs=pl.BlockSpec((None, seq, gw), lambda b, g: (b, 0, g)),
        out_shape=jax.ShapeDtypeStruct((nb, seq, width), BF16),
        scratch_shapes=[
            pltpu.VMEM((seq + 2 * SSD_PAD, gw), F32),
            pltpu.VMEM((seq, gw), F32),
            pltpu.VMEM((seq, st), F32),
            pltpu.VMEM((seq, st), BF16),
            pltpu.VMEM((nchunks, st, CHUNK), BF16),
            pltpu.VMEM((seq, st), BF16),
            pltpu.VMEM((seq, LANES), F32),
            pltpu.VMEM((nchunks, LANES, CHUNK), F32),
            pltpu.VMEM((seq, 2 * LANES), BF16),
            pltpu.VMEM((seq, gw), F32),
            pltpu.VMEM((st, gw), F32),
            pltpu.VMEM((seq, LANES), F32),
            pltpu.VMEM((seq, LANES), F32),
            pltpu.VMEM((seq, LANES), F32),
        ],
        compiler_params=_params(("parallel", "arbitrary")),
        name="ssd",
    )(proj3, proj3, proj3, dt3, conv_w, conv_w, conv_w, conv_b, conv_b, conv_b,
      dtb, alog, dskip, norm_w, expand)


CM_CONV_ROWS = 64
CM_CONV_LANES = 512
CM_SHIFT_ROWS = 32


def _convmod_kernel(v_ref, g_ref, vp_ref, gp_ref, vn_ref, gn_ref, w_ref, b_ref, lg_ref, lb_ref,
                    o_ref, u_s, c_s, sh_s, wb_s):
    tt, width = v_ref.shape
    t = pl.program_id(1)
    nt = pl.num_programs(1)
    half = (CM_CONV - 1) // 2

    def glu(v, g):
        return v * jax.nn.sigmoid(g)

    prev = glu(vp_ref[...], gp_ref[...])
    u_s[0:CM_HALO, :] = jnp.where(t > 0, prev, jnp.zeros_like(prev))
    nxt = glu(vn_ref[...], gn_ref[...])
    u_s[CM_HALO + tt:2 * CM_HALO + tt, :] = jnp.where(t < nt - 1, nxt, jnp.zeros_like(nxt))

    def glu_body(r, carry):
        r0 = pl.multiple_of(r * LN_ROW_BLOCK, LN_ROW_BLOCK)
        rows = pl.ds(r0, LN_ROW_BLOCK)
        u_s[pl.ds(CM_HALO + r0, LN_ROW_BLOCK), :] = glu(v_ref[rows, :], g_ref[rows, :])
        return carry

    lax.fori_loop(0, tt // LN_ROW_BLOCK, glu_body, 0)
    u_s[2 * CM_HALO + tt:2 * CM_HALO + tt + SUBLANES, :] = jnp.zeros((SUBLANES, width), F32)

    for cb in range(width // CM_CONV_LANES):
        lanes = slice(cb * CM_CONV_LANES, (cb + 1) * CM_CONV_LANES)

        def shift_body(r, carry, lanes=lanes):
            r0 = pl.multiple_of(r * CM_SHIFT_ROWS, CM_SHIFT_ROWS)
            win = u_s[pl.ds(r0, CM_SHIFT_ROWS + SUBLANES), lanes]
            for s in range(SUBLANES):
                sh_s[s, pl.ds(r0, CM_SHIFT_ROWS), :] = win[s:s + CM_SHIFT_ROWS, :]
            return carry

        lax.fori_loop(0, (tt + 2 * CM_HALO) // CM_SHIFT_ROWS, shift_body, 0)

        for k in range(CM_CONV):
            wb_s[k] = jnp.broadcast_to(w_ref[k:k + 1, lanes], (SUBLANES, CM_CONV_LANES))

        def conv_body(r, carry, lanes=lanes):
            r0 = pl.multiple_of(r * CM_CONV_ROWS, CM_CONV_ROWS)
            groups = CM_CONV_ROWS // SUBLANES
            acc = jnp.broadcast_to(b_ref[:, lanes], (groups, SUBLANES, CM_CONV_LANES))
            for k in range(CM_CONV):
                off = CM_HALO - half + k
                win = sh_s[off % SUBLANES, pl.ds(r0 + (off // SUBLANES) * SUBLANES, CM_CONV_ROWS), :]
                acc = acc + wb_s[k][None] * win.reshape(groups, SUBLANES, CM_CONV_LANES)
            c_s[pl.ds(r0, CM_CONV_ROWS), lanes] = acc.reshape(CM_CONV_ROWS, CM_CONV_LANES)
            return carry

        lax.fori_loop(0, tt // CM_CONV_ROWS, conv_body, 0)

    lg = lg_ref[...]
    lb = lb_ref[...]

    def ln_body(r, carry):
        rows = pl.ds(pl.multiple_of(r * LN_ROW_BLOCK, LN_ROW_BLOCK), LN_ROW_BLOCK)
        y = c_s[rows, :]
        mu = jnp.mean(y, axis=-1, keepdims=True)
        d = y - mu
        var = jnp.mean(d * d, axis=-1, keepdims=True)
        o_ref[rows, :] = _silu(d * lax.rsqrt(var + LN_EPS) * lg + lb).astype(BF16)
        return carry

    lax.fori_loop(0, tt // LN_ROW_BLOCK, ln_body, 0, unroll=8)


def _convmod(proj3, conv_w, conv_b, ln_g, ln_b, layer, col, rows_per_step):
    nb, seq, _ = proj3.shape
    width = conv_w.shape[-1]
    tt = min(rows_per_step, seq)
    halo_per_tile = tt // CM_HALO
    last_halo = seq // CM_HALO - 1
    v_blk, g_blk = col["cm_val"] // width, col["cm_gate"] // width

    def cur(blk):
        return pl.BlockSpec((None, tt, width), lambda b, t: (b, t, blk))

    def before(blk):
        return pl.BlockSpec((None, CM_HALO, width),
                            lambda b, t: (b, jnp.maximum(t * halo_per_tile - 1, 0), blk))

    def after(blk):
        return pl.BlockSpec((None, CM_HALO, width),
                            lambda b, t: (b, jnp.minimum((t + 1) * halo_per_tile, last_halo), blk))

    def per_layer(rows):
        return pl.BlockSpec((None, rows, width), lambda b, t: (layer, 0, 0))

    return pl.pallas_call(
        _convmod_kernel,
        grid=(nb, seq // tt),
        in_specs=[cur(v_blk), cur(g_blk), before(v_blk), before(g_blk), after(v_blk), after(g_blk),
                  per_layer(CM_CONV), per_layer(1), per_layer(1), per_layer(1)],
        out_specs=pl.BlockSpec((None, tt, width), lambda b, t: (b, t, 0)),
        out_shape=jax.ShapeDtypeStruct((nb, seq, width), BF16),
        scratch_shapes=[
            pltpu.VMEM((tt + 2 * CM_HALO + SUBLANES, width), F32),
            pltpu.VMEM((tt, width), F32),
            pltpu.VMEM((SUBLANES, tt + 2 * CM_HALO, CM_CONV_LANES), F32),
            pltpu.VMEM((CM_CONV, SUBLANES, CM_CONV_LANES), F32),
        ],
        compiler_params=_params(("parallel", "arbitrary")),
        name="convmod",
    )(proj3, proj3, proj3, proj3, proj3, proj3, conv_w, conv_b, ln_g, ln_b)


def _outproj_kernel(xc_ref, a_ref, u_ref, w_ref, g_ref, b_ref, o32_ref, ob_ref, acc_s, *,
                    alpha, ksplit, nk):
    k = pl.program_id(1)

    @pl.when(k == 0)
    def _():
        acc_s[...] = jnp.zeros_like(acc_s)

    @pl.when(k < nk)
    def _():
        lhs = jnp.where(k < ksplit, a_ref[...], u_ref[...])
        acc_s[...] += jnp.dot(lhs, w_ref[...], preferred_element_type=F32)

    _add_residual_chunk(k, xc_ref, acc_s, alpha)

    @pl.when(k >= nk)
    def _():
        _layernorm_slab(k - nk, 1.0, acc_s, g_ref, b_ref, o32_ref, ob_ref)


def _outproj_ln(x32, a, u, w, g, b, layer, alpha):
    t, d = x32.shape
    ka = a.shape[-1]
    tm = min(OUT_ROWS, t)
    tk = OUT_K
    ksplit = ka // tk
    nk = w.shape[1] // tk
    assert nk >= RES_CHUNKS and d % RES_CHUNKS == 0 and tm % (LN_SLABS * LN_ROW_BLOCK) == 0
    cw = d // RES_CHUNKS
    slab = tm // LN_SLABS
    return pl.pallas_call(
        functools.partial(_outproj_kernel, alpha=alpha, ksplit=ksplit, nk=nk),
        grid=(t // tm, nk + LN_SLABS),
        in_specs=[
            pl.BlockSpec((tm, cw), lambda i, k: (i, jnp.minimum(k, RES_CHUNKS - 1))),
            pl.BlockSpec((tm, tk), lambda i, k: (i, jnp.minimum(k, ksplit - 1))),
            pl.BlockSpec((tm, tk), lambda i, k: (i, jnp.clip(k - ksplit, 0, ksplit - 1))),
            pl.BlockSpec((None, tk, d), lambda i, k: (layer, jnp.minimum(k, nk - 1), 0)),
            pl.BlockSpec((None, 1, d), lambda i, k: (layer, 0, 0)),
            pl.BlockSpec((None, 1, d), lambda i, k: (layer, 0, 0)),
        ],
        out_specs=[
            pl.BlockSpec((slab, d), lambda i, k: _slab_index(i, k, nk)),
            pl.BlockSpec((slab, d), lambda i, k: _slab_index(i, k, nk)),
        ],
        out_shape=[jax.ShapeDtypeStruct((t, d), F32), jax.ShapeDtypeStruct((t, d), BF16)],
        scratch_shapes=[pltpu.VMEM((tm, d), F32)],
        compiler_params=_params(("parallel", "arbitrary")),
        name="outproj_ln",
    )(x32, a, u, w, g, b)


def _column_layout(d):
    return {"z": 0, "cm_val": d, "cm_gate": 2 * d, "xs": 3 * d, "bc": 4 * d}


def _group_major(fwd, bwd):
    nl = fwd.shape[0]
    both = jnp.stack([fwd.reshape(nl, GROUPS, HEADS_PER_GROUP),
                      bwd.reshape(nl, GROUPS, HEADS_PER_GROUP)], axis=2)
    return both.reshape(nl, 1, 2 * GROUPS * HEADS_PER_GROUP).astype(F32)


def kernel(x_prompt, x_sample, ffn1_w_gate, ffn1_w_up, ffn1_w_down, ffn1_ln_g, ffn1_ln_b, w_in, ssd_conv_w, ssd_conv_b, dt_bias_fwd, dt_bias_bwd, a_log_fwd, a_log_bwd, d_skip, ssd_norm_w, cm_conv_w, cm_conv_b, cm_ln_g, cm_ln_b, w_out, mix_ln_g, mix_ln_b, ffn2_w_gate, ffn2_w_up, ffn2_w_down, ffn2_ln_g, ffn2_ln_b):
    depth, d, _ = w_in.shape
    alpha = float((2 * depth) ** 0.25)
    nbp, seq, _ = x_prompt.shape
    nbs = x_sample.shape[0]
    assert x_sample.shape[1] == seq and seq % CHUNK == 0
    nb = nbp + nbs
    heads = GROUPS * HEADS_PER_GROUP
    width = GROUPS * GROUP_WIDTH
    bcw = GROUPS * STATE
    xbc = width + 2 * bcw
    assert d == width and w_in.shape[-1] == width + xbc + 2 * heads + 2 * d

    cast = lambda w: w.astype(BF16)
    row = lambda p: p.reshape(depth, 1, -1).astype(F32)
    f1gu, f1d = _interleave_gate_up(ffn1_w_gate, ffn1_w_up), cast(ffn1_w_down)
    f2gu, f2d = _interleave_gate_up(ffn2_w_gate, ffn2_w_up), cast(ffn2_w_down)
    wo = cast(w_out)
    o_xbc, o_dt, o_cm = width, width + xbc, width + xbc + 2 * heads
    o_b, o_c = o_xbc + width, o_xbc + width + bcw
    w_in_b = cast(w_in)
    w_b = w_in_b[:, :, o_b:o_c].reshape(depth, d, GROUPS, STATE)
    w_c = w_in_b[:, :, o_c:o_dt].reshape(depth, d, GROUPS, STATE)
    w_bc = jnp.concatenate([w_b, w_c], axis=-1).reshape(depth, d, 2 * bcw)
    w_main = jnp.concatenate(
        [w_in_b[:, :, :o_xbc], w_in_b[:, :, o_cm:], w_in_b[:, :, o_xbc:o_b], w_bc], axis=-1)
    perm = np.arange(2 * heads).reshape(2, GROUPS, HEADS_PER_GROUP).transpose(1, 0, 2).reshape(-1)
    w_dt = w_in_b[:, :, o_dt:o_cm][:, :, perm]
    dtb = _group_major(dt_bias_fwd, dt_bias_bwd)
    alog = _group_major(a_log_fwd, a_log_bwd)
    dskip = jnp.repeat(d_skip.astype(F32), HEAD_DIM, axis=-1).reshape(depth, 1, width)
    expand = jnp.asarray(_expand_matrix(), BF16)
    col = _column_layout(d)

    def trunk(x):
        nb = x.shape[0]
        x32 = x.reshape(nb * seq, d)
        xb = x32.astype(BF16)
        for l in range(depth):
            x32, xb = _ffn_ln(x32, xb, f1gu, f1d, row(ffn1_ln_g), row(ffn1_ln_b), l, alpha)
            proj, dt_raw = _inproj(xb, w_main, w_dt, l)
            proj3 = proj.reshape(nb, seq, -1)
            ssd_out = _ssd(proj3, dt_raw.reshape(nb, seq, -1), ssd_conv_w, row(ssd_conv_b), dtb, alog,
                           dskip, row(ssd_norm_w), expand, l, col)
            u = _convmod(proj3, cm_conv_w, row(cm_conv_b), row(cm_ln_g), row(cm_ln_b), l, col, CM_ROWS)
            x32, xb = _outproj_ln(x32, ssd_out.reshape(nb * seq, width), u.reshape(nb * seq, d), wo,
                                  row(mix_ln_g), row(mix_ln_b), l, alpha)
            x32, xb = _ffn_ln(x32, xb, f2gu, f2d, row(ffn2_ln_g), row(ffn2_ln_b), l, alpha)
        return x32.reshape(nb, seq, d)

    return (trunk(x_prompt), trunk(x_sample))
```
